```python
import math
import jax, jax.numpy as jnp
from jax import lax
import numpy as np

D_MODEL = 2048
BATCH = 16
SEQ = 2048
DEPTH = 2

GRID_W = 64
CTX_LEN = 256
FFN_HIDDEN = -(-8 * D_MODEL // (3 * 256)) * 256
MIX_WIDTH = D_MODEL
POOL_GROUPS = 4
POOL_WINDOWS = (2, 4, 8, 16)
POOL_WIDTH = MIX_WIDTH // 2
POOL_GROUP_DIM = POOL_WIDTH // POOL_GROUPS
FOURIER_GROUPS = 4
FOURIER_WIDTH = MIX_WIDTH - POOL_WIDTH
FOURIER_GROUP_DIM = FOURIER_WIDTH // FOURIER_GROUPS
DIFF_HEAD_DIM = 128
DIFF_HEADS = D_MODEL // (2 * DIFF_HEAD_DIM)
DIFF_QK_WIDTH = DIFF_HEADS * 2 * DIFF_HEAD_DIM
DIFF_V_WIDTH = DIFF_HEADS * 2 * DIFF_HEAD_DIM
ROPE_THETA = 10000.0
Q_BLOCK = 128
N_EVEN = (DEPTH + 1) // 2
N_ODD = DEPTH // 2
DEEPNORM_ALPHA = (2.0 * DEPTH) ** 0.25
DEEPNORM_BETA = (8.0 * DEPTH) ** -0.25
LN_EPS = 1e-5
RMS_EPS = 1e-5

kernel_name = 'hybrid_pool_fourier_diffattn_dit'


def layer_norm(x, g, b):
    xf = x.astype(jnp.float32)
    mu = jnp.mean(xf, axis=-1, keepdims=True)
    var = jnp.mean(jnp.square(xf - mu), axis=-1, keepdims=True)
    y = (xf - mu) * lax.rsqrt(var + LN_EPS) * g.astype(jnp.float32) + b.astype(jnp.float32)
    return y.astype(x.dtype)


def modulate(x, shift, scale):
    return x * (1.0 + scale) + shift


def swiglu(h, w_gate, w_up, w_down):
    return (jax.nn.silu(h @ w_gate) * (h @ w_up)) @ w_down


def pool_mixer(u, lin, scale):
    b_, n, _ = u.shape
    ug = u.reshape(b_, n, POOL_GROUPS, POOL_GROUP_DIM)
    cs = jnp.cumsum(ug.astype(jnp.float32), axis=1)
    cs = jnp.pad(cs, ((0, 0), (1, 0), (0, 0), (0, 0)))
    t = jnp.arange(n)
    outs = []
    for gi, w in enumerate(POOL_WINDOWS):
        pad = w // 2
        lo_off = -(w // 2)
        hi_off = w - w // 2
        csp = jnp.pad(cs[:, :, gi], ((0, 0), (pad, pad), (0, 0)), mode='edge')
        total = csp[:, hi_off + pad:hi_off + pad + n] - csp[:, lo_off + pad:lo_off + pad + n]
        count = (jnp.minimum(t + hi_off, n) - jnp.maximum(t + lo_off, 0)).astype(jnp.float32)
        outs.append(total / count[None, :, None] - ug[:, :, gi].astype(jnp.float32))
    pooled = jnp.stack(outs, axis=2).astype(u.dtype)
    y = jnp.einsum('bngc,gcd->bngd', pooled, lin)
    return y.reshape(b_, n, POOL_WIDTH) * scale


def fourier_mixer(u, lin):
    b_, n, _ = u.shape
    ug = u.reshape(b_, n, FOURIER_GROUPS, FOURIER_GROUP_DIM)
    f = jnp.fft.fft2(ug.astype(jnp.float32), axes=(1, 3), norm='ortho').real.astype(u.dtype)
    y = jnp.einsum('bngc,gcd->bngd', f, lin)
    return y.reshape(b_, n, FOURIER_WIDTH)


def pool_fourier_layer_mixer(h, w_in, pool_lin, pool_scale, fourier_lin, w_out):
    u = h @ w_in
    ya = pool_mixer(u[..., :POOL_WIDTH], pool_lin, pool_scale)
    yb = fourier_mixer(u[..., POOL_WIDTH:], fourier_lin)
    return jnp.concatenate([ya, yb], axis=-1) @ w_out


def axial_rope_tables(rows):
    row = jnp.repeat(jnp.arange(rows), GRID_W).astype(jnp.float32)
    col = jnp.tile(jnp.arange(GRID_W), rows).astype(jnp.float32)
    axis_dim = DIFF_HEAD_DIM // 2
    inv_freq = ROPE_THETA ** (-jnp.arange(0, axis_dim, 2, dtype=jnp.float32) / axis_dim)
    ang_r = row[:, None] * inv_freq[None, :]
    ang_c = col[:, None] * inv_freq[None, :]
    return (jnp.cos(ang_r), jnp.sin(ang_r), jnp.cos(ang_c), jnp.sin(ang_c))


def rope_rotate(x, cos, sin):
    x1, x2 = jnp.split(x, 2, axis=-1)
    return jnp.concatenate([x1 * cos - x2 * sin, x2 * cos + x1 * sin], axis=-1)


def apply_axial_rope(x, cos_r, sin_r, cos_c, sin_c):
    cr, sr, cc, sc = [t[None, :, None, None, :].astype(x.dtype) for t in (cos_r, sin_r, cos_c, sin_c)]
    half = DIFF_HEAD_DIM // 2
    return jnp.concatenate([rope_rotate(x[..., :half], cr, sr), rope_rotate(x[..., half:], cc, sc)], axis=-1)


def diff_attention(h_lat, h_ctx, rope_tabs, w_in, lam_q1, lam_k1, lam_q2, lam_k2, subln_g, w_out, lambda_init, need_ctx):
    def project(h):
        b_, n, _ = h.shape
        qkv = h @ w_in
        q = qkv[..., :DIFF_QK_WIDTH].reshape(b_, n, DIFF_HEADS, 2, DIFF_HEAD_DIM)
        k = qkv[..., DIFF_QK_WIDTH:2 * DIFF_QK_WIDTH].reshape(b_, n, DIFF_HEADS, 2, DIFF_HEAD_DIM)
        v = qkv[..., 2 * DIFF_QK_WIDTH:].reshape(b_, n, DIFF_HEADS, 2 * DIFF_HEAD_DIM)
        return q, k, v

    q_l, k_l, v_l = project(h_lat)
    q_c, k_c, v_c = project(h_ctx)
    q_l = apply_axial_rope(q_l, *rope_tabs)
    k_l = apply_axial_rope(k_l, *rope_tabs)
    lam = (jnp.exp(jnp.sum(lam_q1.astype(jnp.float32) * lam_k1.astype(jnp.float32)))
           - jnp.exp(jnp.sum(lam_q2.astype(jnp.float32) * lam_k2.astype(jnp.float32))) + lambda_init)
    sm_scale = 1.0 / math.sqrt(DIFF_HEAD_DIM)

    def attend(q, k, v):
        s = jnp.einsum('bqhmd,bkhmd->bhmqk', q, k).astype(jnp.float32) * sm_scale
        p = jax.nn.softmax(s, axis=-1)
        wts = (p[:, :, 0] - lam * p[:, :, 1]).astype(v.dtype)
        o = jnp.einsum('bhqk,bkhe->bqhe', wts, v).astype(jnp.float32)
        o = o * lax.rsqrt(jnp.mean(jnp.square(o), axis=-1, keepdims=True) + RMS_EPS)
        o = o * subln_g.astype(jnp.float32) * (1.0 - lambda_init)
        return o.reshape(o.shape[0], o.shape[1], DIFF_V_WIDTH).astype(v.dtype)

    b_, n = h_lat.shape[0], h_lat.shape[1]
    k_all = jnp.concatenate([k_c, k_l], axis=1)
    v_all = jnp.concatenate([v_c, v_l], axis=1)
    n_blocks = n // Q_BLOCK
    q_blocks = q_l.reshape(b_, n_blocks, Q_BLOCK, DIFF_HEADS, 2, DIFF_HEAD_DIM).swapaxes(0, 1)
    o_blocks = lax.map(lambda qb: attend(qb, k_all, v_all), q_blocks)
    o_lat = o_blocks.swapaxes(0, 1).reshape(b_, n, DIFF_V_WIDTH) @ w_out
    o_ctx = attend(q_c, k_c, v_c) @ w_out if need_ctx else None
    return o_lat, o_ctx


def setup_inputs(seed: int = 0) -> dict:
    key = jax.random.key(seed)
    ks = jax.random.split(key, 32)
    f32 = jnp.float32

    def nrm(k, shape, scale):
        return jax.random.normal(k, shape, f32) * scale

    D = D_MODEL
    F = FFN_HIDDEN
    return {
        'x': nrm(ks[0], (BATCH, SEQ, D), 1.0),
        'c': nrm(ks[1], (BATCH, D), 1.0),
        'ctx': nrm(ks[2], (BATCH, CTX_LEN, D), 1.0),
        'c_ctx': nrm(ks[3], (D,), 1.0),
        'w_mod': nrm(ks[4], (DEPTH, D, 6 * D), D ** -0.5),
        'b_mod': nrm(ks[5], (DEPTH, 6 * D), 0.02),
        'ln1_g': 1.0 + nrm(ks[6], (DEPTH, D), 0.05),
        'ln1_b': nrm(ks[7], (DEPTH, D), 0.02),
        'ln2_g': 1.0 + nrm(ks[8], (DEPTH, D), 0.05),
        'ln2_b': nrm(ks[9], (DEPTH, D), 0.02),
        'ffn_w_gate': nrm(ks[10], (DEPTH, D, F), D ** -0.5),
        'ffn_w_up': nrm(ks[11], (DEPTH, D, F), D ** -0.5),
        'ffn_w_down': nrm(ks[12], (DEPTH, F, D), F ** -0.5 * DEEPNORM_BETA),
        'pf_w_in': nrm(ks[13], (N_EVEN, D, MIX_WIDTH), D ** -0.5),
        'pool_lin': nrm(ks[14], (N_EVEN, POOL_GROUPS, POOL_GROUP_DIM, POOL_GROUP_DIM), POOL_GROUP_DIM ** -0.5),
        'pool_scale': 1.0 + nrm(ks[15], (N_EVEN, POOL_WIDTH), 0.05),
        'fourier_lin': nrm(ks[16], (N_EVEN, FOURIER_GROUPS, FOURIER_GROUP_DIM, FOURIER_GROUP_DIM), FOURIER_GROUP_DIM ** -0.5),
        'pf_w_out': nrm(ks[17], (N_EVEN, MIX_WIDTH, D), MIX_WIDTH ** -0.5 * DEEPNORM_BETA),
        'da_w_in': nrm(ks[18], (N_ODD, D, 2 * DIFF_QK_WIDTH + DIFF_V_WIDTH), D ** -0.5),
        'da_lam_q1': nrm(ks[19], (N_ODD, DIFF_HEAD_DIM), 0.1),
        'da_lam_k1': nrm(ks[20], (N_ODD, DIFF_HEAD_DIM), 0.1),
        'da_lam_q2': nrm(ks[21], (N_ODD, DIFF_HEAD_DIM), 0.1),
        'da_lam_k2': nrm(ks[22], (N_ODD, DIFF_HEAD_DIM), 0.1),
        'da_subln_g': 1.0 + nrm(ks[23], (N_ODD, 2 * DIFF_HEAD_DIM), 0.05),
        'da_w_out': nrm(ks[24], (N_ODD, DIFF_V_WIDTH, D), DIFF_V_WIDTH ** -0.5 * DEEPNORM_BETA),
    }


def reference(x, c, ctx, c_ctx, w_mod, b_mod, ln1_g, ln1_b, ln2_g, ln2_b, ffn_w_gate, ffn_w_up, ffn_w_down,
              pf_w_in, pool_lin, pool_scale, fourier_lin, pf_w_out,
              da_w_in, da_lam_q1, da_lam_k1, da_lam_q2, da_lam_k2, da_subln_g, da_w_out):
    rows = x.shape[1] // GRID_W
    rope_tabs = axial_rope_tables(rows)
    silu_c = jax.nn.silu(c)
    silu_cc = jax.nn.silu(c_ctx)
    for l in range(DEPTH):
        last = l == DEPTH - 1
        j = l // 2
        mod = (silu_c @ w_mod[l] + b_mod[l])[:, None, :]
        mod_c = silu_cc @ w_mod[l] + b_mod[l]
        sh1, sc1, g1, sh2, sc2, g2 = jnp.split(mod, 6, axis=-1)
        csh1, csc1, cg1, csh2, csc2, cg2 = jnp.split(mod_c, 6, axis=-1)
        hx = modulate(x, sh1, sc1)
        hc = modulate(ctx, csh1, csc1)
        if l % 2 == 0:
            mx = pool_fourier_layer_mixer(hx, pf_w_in[j], pool_lin[j], pool_scale[j], fourier_lin[j], pf_w_out[j])
            mc = None if last else pool_fourier_layer_mixer(hc, pf_w_in[j], pool_lin[j], pool_scale[j], fourier_lin[j], pf_w_out[j])
        else:
            lambda_init = 0.8 - 0.6 * math.exp(-0.3 * l)
            mx, mc = diff_attention(hx, hc, rope_tabs, da_w_in[j], da_lam_q1[j], da_lam_k1[j], da_lam_q2[j],
                                    da_lam_k2[j], da_subln_g[j], da_w_out[j], lambda_init, not last)
        x = layer_norm(DEEPNORM_ALPHA * x + g1 * mx, ln1_g[l], ln1_b[l])
        fx = swiglu(modulate(x, sh2, sc2), ffn_w_gate[l], ffn_w_up[l], ffn_w_down[l])
        x = layer_norm(DEEPNORM_ALPHA * x + g2 * fx, ln2_g[l], ln2_b[l])
        if not last:
            ctx = layer_norm(DEEPNORM_ALPHA * ctx + cg1 * mc, ln1_g[l], ln1_b[l])
            fc = swiglu(modulate(ctx, csh2, csc2), ffn_w_gate[l], ffn_w_up[l], ffn_w_down[l])
            ctx = layer_norm(DEEPNORM_ALPHA * ctx + cg2 * fc, ln2_g[l], ln2_b[l])
    return x
```

```python
import functools
import math

import jax
import jax.numpy as jnp
from jax import lax
from jax.experimental import pallas as pl
from jax.experimental.pallas import tpu as pltpu

F32 = jnp.float32
BF16 = jnp.bfloat16

D_MODEL = 2048
DEPTH = 2
GRID_W = 64
CTX_LEN = 256
FFN_HIDDEN = 5632
POOL_WINDOWS = (2, 4, 8, 16)
POOL_WIDTH = 1024
GROUP_DIM = 256
N_GROUPS = 4
HEAD_DIM = 128
N_HEADS = 8
HEAD_V = 2 * HEAD_DIM
QK_WIDTH = 2048
ROPE_THETA = 10000.0
DEEPNORM_ALPHA = (2.0 * DEPTH) ** 0.25
LN_EPS = 1e-5
RMS_EPS = 1e-5
N_MOD = 6
MOD_ROWS = 24

V7X_LANES = 128
V7X_VMEM_BYTES = 64 * 1024 * 1024
MIB = 1024 * 1024


def _params(vmem_mib, n_grid):
    assert vmem_mib * MIB < V7X_VMEM_BYTES
    return pltpu.CompilerParams(
        dimension_semantics=("arbitrary",) * n_grid,
        vmem_limit_bytes=vmem_mib * MIB,
    )


def _mod_spec(k, tm, seq_rows, const_row):
    if const_row is None:
        return pl.BlockSpec((1, 1, D_MODEL), lambda i, *_: ((i * tm) // seq_rows * N_MOD + k, 0, 0))
    return pl.BlockSpec((1, 1, D_MODEL), lambda i, *_: (const_row * N_MOD + k, 0, 0))


def _layer_norm(r, g, b):
    mu = jnp.mean(r, axis=-1, keepdims=True)
    d = r - mu
    var = jnp.mean(d * d, axis=-1, keepdims=True)
    return d * lax.rsqrt(var + LN_EPS) * g + b


def _mod_kernel(c_ref, w_ref, b_ref, o_ref):
    c = c_ref[...]
    s = c * jax.nn.sigmoid(c)
    o_ref[...] = jnp.dot(s, w_ref[...], preferred_element_type=F32) + b_ref[...]


def _modulation(c_all, w, b):
    tn = 1024
    n = w.shape[1]
    return pl.pallas_call(
        _mod_kernel,
        grid=(n // tn,),
        in_specs=[
            pl.BlockSpec((MOD_ROWS, D_MODEL), lambda j: (0, 0)),
            pl.BlockSpec((D_MODEL, tn), lambda j: (0, j)),
            pl.BlockSpec((1, tn), lambda j: (0, j)),
        ],
        out_specs=pl.BlockSpec((MOD_ROWS, tn), lambda j: (0, j)),
        out_shape=jax.ShapeDtypeStruct((MOD_ROWS, n), F32),
        compiler_params=_params(24, 1),
        name="modulation",
    )(c_all, w, b.reshape(1, n))


def _rope(acc, cos, sa, sb):
    outs = []
    for c0 in range(0, acc.shape[1], HEAD_DIM):
        xc = acc[:, c0:c0 + HEAD_DIM]
        up = pltpu.roll(xc, HEAD_DIM - 32, 1)
        dn = pltpu.roll(xc, 32, 1)
        outs.append(xc * cos + up * sa + dn * sb)
    return jnp.concatenate(outs, axis=1)


def _modmm_kernel(*refs, rope_tiles, q_tiles, q_scale):
    if rope_tiles:
        x_ref, sh_ref, sc_ref, w_ref, cos_ref, sa_ref, sb_ref, o_ref, h_ref = refs
    else:
        x_ref, sh_ref, sc_ref, w_ref, o_ref, h_ref = refs
    j = pl.program_id(1)

    @pl.when(j == 0)
    def _():
        h_ref[...] = (x_ref[...] * (1.0 + sc_ref[0]) + sh_ref[0]).astype(BF16)

    acc = jnp.dot(h_ref[...], w_ref[...], preferred_element_type=F32)
    if not rope_tiles:
        o_ref[...] = acc.astype(o_ref.dtype)
        return

    @pl.when(j < rope_tiles)
    def _():
        fac = jnp.where(j < q_tiles, q_scale, 1.0).astype(F32)
        o_ref[...] = (_rope(acc, cos_ref[...], sa_ref[...], sb_ref[...]) * fac).astype(o_ref.dtype)

    @pl.when(j >= rope_tiles)
    def _():
        o_ref[...] = acc.astype(o_ref.dtype)


def _modulated_matmul(x, mod, w, *, shift_k, scale_k, seq_rows, const_row, rope=None, out_dtype=BF16):
    t, n = x.shape[0], w.shape[1]
    tm, tn = 1024, 1024
    in_specs = [
        pl.BlockSpec((tm, D_MODEL), lambda i, j: (i, 0)),
        _mod_spec(shift_k, tm, seq_rows, const_row),
        _mod_spec(scale_k, tm, seq_rows, const_row),
        pl.BlockSpec((D_MODEL, tn), lambda i, j: (0, j)),
    ]
    args = [x, mod, mod, w]
    kw = dict(rope_tiles=0, q_tiles=0, q_scale=1.0)
    if rope is not None:
        tiles_per_seq = seq_rows // tm
        tab_spec = pl.BlockSpec((tm, HEAD_DIM), lambda i, j: (i % tiles_per_seq, 0))
        in_specs += [tab_spec] * 3
        args += list(rope)
        kw = dict(rope_tiles=2 * QK_WIDTH // tn, q_tiles=QK_WIDTH // tn, q_scale=1.0 / math.sqrt(HEAD_DIM))
    return pl.pallas_call(
        functools.partial(_modmm_kernel, **kw),
        grid=(t // tm, n // tn),
        in_specs=in_specs,
        out_specs=pl.BlockSpec((tm, tn), lambda i, j: (i, j)),
        out_shape=jax.ShapeDtypeStruct((t, n), out_dtype),
        scratch_shapes=[pltpu.VMEM((tm, D_MODEL), BF16)],
        compiler_params=_params(48, 2),
        name="modulated_matmul",
    )(*args)


POOL_PAD = 16
POOL_CHUNK = 256


def _pool_kernel(u_ref, lin_ref, scale_ref, o_ref, pad_ref, *, n):
    zeros = jnp.zeros((POOL_PAD, GROUP_DIM), F32)
    pad_ref[0:POOL_PAD, :] = zeros
    pad_ref[POOL_PAD + n:2 * POOL_PAD + n, :] = zeros
    for g, w in enumerate(POOL_WINDOWS):
        half = w // 2
        cols = slice(g * GROUP_DIM, (g + 1) * GROUP_DIM)
        pad_ref[POOL_PAD:POOL_PAD + n, :] = u_ref[:, cols].astype(F32)
        for r0 in range(0, n, POOL_CHUNK):
            tot = pad_ref[POOL_PAD + r0 - half:POOL_PAD + r0 - half + POOL_CHUNK, :]
            for d in range(-half + 1, half):
                tot = tot + pad_ref[POOL_PAD + r0 + d:POOL_PAD + r0 + d + POOL_CHUNK, :]
            t = r0 + lax.broadcasted_iota(jnp.int32, (POOL_CHUNK, 1), 0)
            cnt = (jnp.minimum(t + half, n) - jnp.maximum(t - half, 0)).astype(F32)
            pooled = tot / cnt - pad_ref[POOL_PAD + r0:POOL_PAD + r0 + POOL_CHUNK, :]
            y = jnp.dot(pooled.astype(BF16), lin_ref[g], preferred_element_type=F32) * scale_ref[g]
            o_ref[r0:r0 + POOL_CHUNK, cols] = y.astype(o_ref.dtype)


def _pool_mixer(u, lin, scale, n):
    t = u.shape[0]
    return pl.pallas_call(
        functools.partial(_pool_kernel, n=n),
        grid=(t // n,),
        in_specs=[
            pl.BlockSpec((n, POOL_WIDTH), lambda b: (b, 0)),
            pl.BlockSpec((N_GROUPS, GROUP_DIM, GROUP_DIM), lambda b: (0, 0, 0)),
            pl.BlockSpec((N_GROUPS, 1, GROUP_DIM), lambda b: (0, 0, 0)),
        ],
        out_specs=pl.BlockSpec((n, POOL_WIDTH), lambda b: (b, 0)),
        out_shape=jax.ShapeDtypeStruct((t, POOL_WIDTH), BF16),
        scratch_shapes=[pltpu.VMEM((n + 2 * POOL_PAD, GROUP_DIM), F32)],
        compiler_params=_params(32, 1),
        name="pool_mixer",
    )(u, lin, scale.reshape(N_GROUPS, 1, GROUP_DIM))


def _dft_tables(n):
    k = jnp.arange(n, dtype=jnp.int32)
    ang = ((k[:, None] * k[None, :]) % n).astype(F32) * (2.0 * math.pi / n)
    return jnp.cos(ang).astype(BF16), jnp.sin(ang).astype(BF16)


def _chan_dft_kernel(u_ref, cs_ref, a_ref, b_ref):
    for g in range(N_GROUPS):
        cols = slice(g * GROUP_DIM, (g + 1) * GROUP_DIM)
        ab = jnp.dot(u_ref[:, cols], cs_ref[...], preferred_element_type=F32)
        a_ref[:, cols] = ab[:, :GROUP_DIM].astype(a_ref.dtype)
        b_ref[:, cols] = ab[:, GROUP_DIM:].astype(b_ref.dtype)


def _chan_dft(u, cs):
    t = u.shape[0]
    tm = 1024
    width = N_GROUPS * GROUP_DIM
    out = jax.ShapeDtypeStruct((t, width), BF16)
    return pl.pallas_call(
        _chan_dft_kernel,
        grid=(t // tm,),
        in_specs=[
            pl.BlockSpec((tm, width), lambda i: (i, 1)),
            pl.BlockSpec((GROUP_DIM, 2 * GROUP_DIM), lambda i: (0, 0)),
        ],
        out_specs=[pl.BlockSpec((tm, width), lambda i: (i, 0))] * 2,
        out_shape=[out, out],
        compiler_params=_params(32, 1),
        name="channel_dft",
    )(u, cs)


def _seq_dft_kernel(c_ref, s_ref, a_ref, b_ref, lin_ref, o_ref, *, norm):
    z = (jnp.dot(c_ref[...], a_ref[...], preferred_element_type=F32)
         - jnp.dot(s_ref[...], b_ref[...], preferred_element_type=F32)) * norm
    zb = z.astype(BF16)
    for g in range(N_GROUPS):
        cols = slice(g * GROUP_DIM, (g + 1) * GROUP_DIM)
        o_ref[:, cols] = jnp.dot(zb[:, cols], lin_ref[g], preferred_element_type=F32).astype(o_ref.dtype)


def _seq_dft(cn, sn, a, b, lin, n):
    t = a.shape[0]
    tm = min(512, n)
    width = N_GROUPS * GROUP_DIM
    tiles = n // tm
    return pl.pallas_call(
        functools.partial(_seq_dft_kernel, norm=1.0 / math.sqrt(n * GROUP_DIM)),
        grid=(t // n, tiles),
        in_specs=[
            pl.BlockSpec((tm, n), lambda b_, i: (i, 0)),
            pl.BlockSpec((tm, n), lambda b_, i: (i, 0)),
            pl.BlockSpec((n, width), lambda b_, i: (b_, 0)),
            pl.BlockSpec((n, width), lambda b_, i: (b_, 0)),
            pl.BlockSpec((N_GROUPS, GROUP_DIM, GROUP_DIM), lambda b_, i: (0, 0, 0)),
        ],
        out_specs=pl.BlockSpec((tm, width), lambda b_, i: (b_ * tiles + i, 0)),
        out_shape=jax.ShapeDtypeStruct((t, width), BF16),
        compiler_params=_params(40, 2),
        name="sequence_dft",
    )(cn, sn, a, b, lin)


def _proj_ln_kernel(*refs, n_parts):
    y_refs = refs[:n_parts]
    w_refs = refs[n_parts:2 * n_parts]
    x_ref, g_ref, lng_ref, lnb_ref, o_ref = refs[2 * n_parts:]
    mx = jnp.dot(y_refs[0][...], w_refs[0][...], preferred_element_type=F32)
    for y_ref, w_ref in zip(y_refs[1:], w_refs[1:]):
        mx = mx + jnp.dot(y_ref[...], w_ref[...], preferred_element_type=F32)
    r = DEEPNORM_ALPHA * x_ref[...] + g_ref[0] * mx
    o_ref[...] = _layer_norm(r, lng_ref[...], lnb_ref[...])


def _proj_ln(ys, ws, x, mod, ln_g, ln_b, *, gate_k, seq_rows, const_row):
    t = x.shape[0]
    tm = 512
    n_parts = len(ys)
    in_specs = [pl.BlockSpec((tm, y.shape[1]), lambda i: (i, 0)) for y in ys]
    in_specs += [pl.BlockSpec(w.shape, lambda i: (0, 0)) for w in ws]
    in_specs += [
        pl.BlockSpec((tm, D_MODEL), lambda i: (i, 0)),
        _mod_spec(gate_k, tm, seq_rows, const_row),
        pl.BlockSpec((1, D_MODEL), lambda i: (0, 0)),
        pl.BlockSpec((1, D_MODEL), lambda i: (0, 0)),
    ]
    return pl.pallas_call(
        functools.partial(_proj_ln_kernel, n_parts=n_parts),
        grid=(t // tm,),
        in_specs=in_specs,
        out_specs=pl.BlockSpec((tm, D_MODEL), lambda i: (i, 0)),
        out_shape=jax.ShapeDtypeStruct((t, D_MODEL), F32),
        compiler_params=_params(48, 1),
        name="proj_layer_norm",
    )(*ys, *ws, x, mod, ln_g.reshape(1, D_MODEL), ln_b.reshape(1, D_MODEL))


def _ffn_kernel(x_ref, sh_ref, sc_ref, g_ref, wg_ref, wu_ref, wd_ref, lng_ref, lnb_ref, o_ref, h_ref):
    f = pl.program_id(1)

    @pl.when(f == 0)
    def _():
        h_ref[...] = (x_ref[...] * (1.0 + sc_ref[0]) + sh_ref[0]).astype(BF16)

    h = h_ref[...]
    gate = jnp.dot(h, wg_ref[...], preferred_element_type=F32)
    up = jnp.dot(h, wu_ref[...], preferred_element_type=F32)
    act = (gate * jax.nn.sigmoid(gate) * up).astype(BF16)
    part = jnp.dot(act, wd_ref[...], preferred_element_type=F32)

    @pl.when(f == 0)
    def _():
        o_ref[...] = part

    @pl.when(f > 0)
    def _():
        o_ref[...] += part

    @pl.when(f == pl.num_programs(1) - 1)
    def _():
        r = DEEPNORM_ALPHA * x_ref[...] + g_ref[0] * o_ref[...]
        o_ref[...] = _layer_norm(r, lng_ref[...], lnb_ref[...])


def _ffn_ln(x, mod, wg, wu, wd, ln_g, ln_b, *, seq_rows, const_row):
    t = x.shape[0]
    tm, tf = 512, 512
    row = lambda i, f: (i, 0)
    vec = pl.BlockSpec((1, D_MODEL), lambda i, f: (0, 0))
    return pl.pallas_call(
        _ffn_kernel,
        grid=(t // tm, FFN_HIDDEN // tf),
        in_specs=[
            pl.BlockSpec((tm, D_MODEL), row),
            _mod_spec(3, tm, seq_rows, const_row),
            _mod_spec(4, tm, seq_rows, const_row),
            _mod_spec(5, tm, seq_rows, const_row),
            pl.BlockSpec((D_MODEL, tf), lambda i, f: (0, f)),
            pl.BlockSpec((D_MODEL, tf), lambda i, f: (0, f)),
            pl.BlockSpec((tf, D_MODEL), lambda i, f: (f, 0)),
            vec,
            vec,
        ],
        out_specs=pl.BlockSpec((tm, D_MODEL), row),
        out_shape=jax.ShapeDtypeStruct((t, D_MODEL), F32),
        scratch_shapes=[pltpu.VMEM((tm, D_MODEL), BF16)],
        compiler_params=_params(48, 2),
        name="swiglu_layer_norm",
    )(x, mod, mod, mod, wg, wu, wd, ln_g.reshape(1, D_MODEL), ln_b.reshape(1, D_MODEL))


def _rope_tables(n):
    rows = n // GRID_W
    row = jnp.repeat(jnp.arange(rows), GRID_W).astype(F32)
    col = jnp.tile(jnp.arange(GRID_W), rows).astype(F32)
    axis_dim = HEAD_DIM // 2
    inv_freq = ROPE_THETA ** (-jnp.arange(0, axis_dim, 2, dtype=F32) / axis_dim)
    ang_r = row[:, None] * inv_freq[None, :]
    ang_c = col[:, None] * inv_freq[None, :]
    cr, sr, cc, sc = jnp.cos(ang_r), jnp.sin(ang_r), jnp.cos(ang_c), jnp.sin(ang_c)
    zero = jnp.zeros_like(sr)
    cos = jnp.concatenate([cr, cr, cc, cc], axis=-1)
    sa = jnp.concatenate([-sr, zero, -sc, zero], axis=-1)
    sb = jnp.concatenate([zero, sr, zero, sc], axis=-1)
    return cos, sa, sb


def _attn_kernel(q_ref, kl_ref, vl_ref, kc_ref, vc_ref, lq1_ref, lk1_ref, lq2_ref, lk2_ref, g_ref,
                 o_ref, k_all, v_all, *, lambda_init):
    @pl.when(pl.program_id(2) == 0)
    def _():
        k_all[0:CTX_LEN, :] = kc_ref[...]
        k_all[CTX_LEN:, :] = kl_ref[...]
        v_all[0:CTX_LEN, :] = vc_ref[...]
        v_all[CTX_LEN:, :] = vl_ref[...]

    lam = (jnp.exp(jnp.sum(lq1_ref[...] * lk1_ref[...], axis=-1, keepdims=True))
           - jnp.exp(jnp.sum(lq2_ref[...] * lk2_ref[...], axis=-1, keepdims=True)) + lambda_init)
    es, inv = [], []
    for m in range(2):
        lanes = slice(m * HEAD_DIM, (m + 1) * HEAD_DIM)
        s = lax.dot_general(q_ref[:, lanes], k_all[:, lanes], (((1,), (1,)), ((), ())),
                            preferred_element_type=F32)
        e = jnp.exp(s - jnp.max(s, axis=-1, keepdims=True))
        es.append(e)
        inv.append(1.0 / jnp.sum(e, axis=-1, keepdims=True))
    wts = (es[0] * inv[0] - es[1] * (lam * inv[1])).astype(BF16)
    o = jnp.dot(wts, v_all[...], preferred_element_type=F32)
    o = o * lax.rsqrt(jnp.mean(o * o, axis=-1, keepdims=True) + RMS_EPS)
    o_ref[...] = (o * g_ref[...] * (1.0 - lambda_init)).astype(o_ref.dtype)


def _diff_attention(qkv, kvc, lam_vecs, subln_g, n, lambda_init):
    t = qkv.shape[0]
    tq = 256
    nq = n // tq
    k_blk = QK_WIDTH // HEAD_V
    v_blk = 2 * QK_WIDTH // HEAD_V
    vec = pl.BlockSpec((1, HEAD_DIM), lambda b, h, i: (0, 0))
    return pl.pallas_call(
        functools.partial(_attn_kernel, lambda_init=lambda_init),
        grid=(t // n, N_HEADS, nq),
        in_specs=[
            pl.BlockSpec((tq, HEAD_V), lambda b, h, i: (b * nq + i, h)),
            pl.BlockSpec((n, HEAD_V), lambda b, h, i: (b, k_blk + h)),
            pl.BlockSpec((n, HEAD_V), lambda b, h, i: (b, v_blk + h)),
            pl.BlockSpec((CTX_LEN, HEAD_V), lambda b, h, i: (b, h)),
            pl.BlockSpec((CTX_LEN, HEAD_V), lambda b, h, i: (b, N_HEADS + h)),
            vec, vec, vec, vec,
            pl.BlockSpec((1, HEAD_V), lambda b, h, i: (0, 0)),
        ],
        out_specs=pl.BlockSpec((tq, HEAD_V), lambda b, h, i: (b * nq + i, h)),
        out_shape=jax.ShapeDtypeStruct((t, N_HEADS * HEAD_V), BF16),
        scratch_shapes=[pltpu.VMEM((CTX_LEN + n, HEAD_V), BF16), pltpu.VMEM((CTX_LEN + n, HEAD_V), BF16)],
        compiler_params=_params(40, 3),
        name="diff_attention",
    )(qkv, qkv, qkv, kvc, kvc, *[v.reshape(1, HEAD_DIM) for v in lam_vecs], subln_g.reshape(1, HEAD_V))


def _pool_fourier_mixer(h_src, mod, w_in, pool_lin, pool_scale, fourier_lin, chan_tab, seq_tabs, n, const_row):
    u = _modulated_matmul(h_src, mod, w_in, shift_k=0, scale_k=1, seq_rows=n, const_row=const_row)
    ya = _pool_mixer(u, pool_lin, pool_scale, n)
    a, b = _chan_dft(u, chan_tab)
    yb = _seq_dft(seq_tabs[0], seq_tabs[1], a, b, fourier_lin, n)
    return ya, yb


def kernel(x, c, ctx, c_ctx, w_mod, b_mod, ln1_g, ln1_b, ln2_g, ln2_b, ffn_w_gate, ffn_w_up, ffn_w_down,
           pf_w_in, pool_lin, pool_scale, fourier_lin, pf_w_out,
           da_w_in, da_lam_q1, da_lam_k1, da_lam_q2, da_lam_k2, da_subln_g, da_w_out):
    batch, n, _ = x.shape
    n_ctx = ctx.shape[1]
    ctx_row = batch
    xs = x.reshape(batch * n, D_MODEL)
    cs = ctx.reshape(batch * n_ctx, D_MODEL)
    c_all = jnp.zeros((MOD_ROWS, D_MODEL), F32).at[:batch].set(c).at[ctx_row].set(c_ctx)

    chan_cos, chan_sin = _dft_tables(GROUP_DIM)
    chan_tab = jnp.concatenate([chan_cos, chan_sin], axis=1)
    seq_tabs = {n: _dft_tables(n), n_ctx: _dft_tables(n_ctx)}
    rope_tabs = _rope_tables(n)

    for l in range(DEPTH):
        last = l == DEPTH - 1
        j = l // 2
        mod = _modulation(c_all, w_mod[l], b_mod[l]).reshape(MOD_ROWS * N_MOD, 1, D_MODEL)
        wg, wu, wd = ffn_w_gate[l].astype(BF16), ffn_w_up[l].astype(BF16), ffn_w_down[l].astype(BF16)
        streams = [(xs, n, None)] + ([] if last else [(cs, n_ctx, ctx_row)])
        if l % 2 == 0:
            w_in = pf_w_in[j].astype(BF16)
            w_out = pf_w_out[j].astype(BF16)
            p_lin, f_lin = pool_lin[j].astype(BF16), fourier_lin[j].astype(BF16)
            mixed = []
            for src, rows, const_row in streams:
                ya, yb = _pool_fourier_mixer(src, mod, w_in, p_lin, pool_scale[j], f_lin, chan_tab,
                                             seq_tabs[rows], rows, const_row)
                mixed.append(([ya, yb], [w_out[:POOL_WIDTH], w_out[POOL_WIDTH:]]))
        else:
            lambda_init = 0.8 - 0.6 * math.exp(-0.3 * l)
            w_in = da_w_in[j].astype(BF16)
            w_out = da_w_out[j].astype(BF16)
            qkv = _modulated_matmul(xs, mod, w_in, shift_k=0, scale_k=1, seq_rows=n, const_row=None,
                                    rope=rope_tabs)
            kvc = _modulated_matmul(cs, mod, w_in[:, QK_WIDTH:], shift_k=0, scale_k=1, seq_rows=n_ctx,
                                    const_row=ctx_row)
            o_lat = _diff_attention(qkv, kvc, (da_lam_q1[j], da_lam_k1[j], da_lam_q2[j], da_lam_k2[j]),
                                    da_subln_g[j], n, lambda_init)
            assert last, "context queries are only needed when a later layer consumes the context stream"
            mixed = [([o_lat], [w_out])]
        new_streams = []
        for (src, rows, const_row), (ys, ws) in zip(streams, mixed):
            mid = _proj_ln(ys, ws, src, mod, ln1_g[l], ln1_b[l], gate_k=2, seq_rows=rows, const_row=const_row)
            new_streams.append(_ffn_ln(mid, mod, wg, wu, wd, ln2_g[l], ln2_b[l], seq_rows=rows,
                                       const_row=const_row))
        xs = new_streams[0]
        if not last:
            cs = new_streams[1]
    return xs.reshape(batch, n, D_MODEL)
```

```python
import functools
import math

import jax
import jax.numpy as jnp
from jax import lax
from jax.experimental import pallas as pl
from jax.experimental.pallas import tpu as pltpu

F32 = jnp.float32
BF16 = jnp.bfloat16

D_MODEL = 2048
DEPTH = 2
GRID_W = 64
CTX_LEN = 256
FFN_HIDDEN = 5632
POOL_WINDOWS = (2, 4, 8, 16)
POOL_WIDTH = 1024
GROUP_DIM = 256
N_GROUPS = 4
HEAD_DIM = 128
N_HEADS = 8
HEAD_V = 2 * HEAD_DIM
QK_WIDTH = 2048
ROPE_THETA = 10000.0
DEEPNORM_ALPHA = (2.0 * DEPTH) ** 0.25
LN_EPS = 1e-5
RMS_EPS = 1e-5
N_MOD = 6
MOD_ROWS = 24

V7X_MXU_DIM = 256
V7X_VMEM_BYTES = 64 * 1024 * 1024
MIB = 1024 * 1024

ATTN_TQ = 256


def _params(vmem_mib, n_grid):
    assert vmem_mib * MIB < V7X_VMEM_BYTES
    return pltpu.CompilerParams(
        dimension_semantics=("arbitrary",) * n_grid,
        vmem_limit_bytes=vmem_mib * MIB,
    )


def _mod_spec(k, tm, seq_rows, const_row):
    if const_row is None:
        return pl.BlockSpec((1, 1, D_MODEL), lambda i, *_: ((i * tm) // seq_rows * N_MOD + k, 0, 0))
    return pl.BlockSpec((1, 1, D_MODEL), lambda i, *_: (const_row * N_MOD + k, 0, 0))


def _layer_norm(r, g, b):
    mu = jnp.mean(r, axis=-1, keepdims=True)
    d = r - mu
    var = jnp.mean(d * d, axis=-1, keepdims=True)
    return d * lax.rsqrt(var + LN_EPS) * g + b


def _mod_kernel(c_ref, w_ref, b_ref, o_ref):
    c = c_ref[...]
    s = c * jax.nn.sigmoid(c)
    o_ref[...] = jnp.dot(s, w_ref[...], preferred_element_type=F32) + b_ref[...]


def _modulation(c_all, w, b):
    tn = 1024
    n = w.shape[1]
    return pl.pallas_call(
        _mod_kernel,
        grid=(n // tn,),
        in_specs=[
            pl.BlockSpec((MOD_ROWS, D_MODEL), lambda j: (0, 0)),
            pl.BlockSpec((D_MODEL, tn), lambda j: (0, j)),
            pl.BlockSpec((1, tn), lambda j: (0, j)),
        ],
        out_specs=pl.BlockSpec((MOD_ROWS, tn), lambda j: (0, j)),
        out_shape=jax.ShapeDtypeStruct((MOD_ROWS, n), F32),
        compiler_params=_params(24, 1),
        name="modulation",
    )(c_all, w, b.reshape(1, n))


def _rope(acc, cos, sa, sb):
    outs = []
    for c0 in range(0, acc.shape[1], HEAD_DIM):
        xc = acc[:, c0:c0 + HEAD_DIM]
        up = pltpu.roll(xc, HEAD_DIM - 32, 1)
        dn = pltpu.roll(xc, 32, 1)
        outs.append(xc * cos + up * sa + dn * sb)
    return jnp.concatenate(outs, axis=1)


def _modmm_kernel(*refs, rope, q_tiles, q_scale):
    if rope:
        x_ref, sh_ref, sc_ref, w_ref, cos_ref, sa_ref, sb_ref, o_ref, h_ref = refs
    else:
        x_ref, sh_ref, sc_ref, w_ref, o_ref, h_ref = refs
    j = pl.program_id(1)

    @pl.when(j == 0)
    def _():
        h_ref[...] = (x_ref[...] * (1.0 + sc_ref[0]) + sh_ref[0]).astype(BF16)

    h = h_ref[...]
    if rope:
        fac = jnp.where(j < q_tiles, q_scale, 1.0).astype(F32)
        cos, sa, sb = cos_ref[...] * fac, sa_ref[...] * fac, sb_ref[...] * fac
    for c0 in range(0, o_ref.shape[1], V7X_MXU_DIM):
        acc = jnp.dot(h, w_ref[:, c0:c0 + V7X_MXU_DIM], preferred_element_type=F32)
        if rope:
            acc = _rope(acc, cos, sa, sb)
        o_ref[:, c0:c0 + V7X_MXU_DIM] = acc.astype(o_ref.dtype)


def _modulated_matmul(x, mod, w, *, shift_k, scale_k, seq_rows, const_row, rope=None, q_cols=0, q_scale=1.0):
    t, n = x.shape[0], w.shape[1]
    tm, tn = 1024, 1024
    in_specs = [
        pl.BlockSpec((tm, D_MODEL), lambda i, j: (i, 0)),
        _mod_spec(shift_k, tm, seq_rows, const_row),
        _mod_spec(scale_k, tm, seq_rows, const_row),
        pl.BlockSpec((D_MODEL, tn), lambda i, j: (0, j)),
    ]
    args = [x, mod, mod, w]
    if rope is not None:
        tiles_per_seq = seq_rows // tm
        tab_spec = pl.BlockSpec((tm, HEAD_DIM), lambda i, j: (i % tiles_per_seq, 0))
        in_specs += [tab_spec] * 3
        args += list(rope)
    return pl.pallas_call(
        functools.partial(_modmm_kernel, rope=rope is not None, q_tiles=q_cols // tn, q_scale=q_scale),
        grid=(t // tm, n // tn),
        in_specs=in_specs,
        out_specs=pl.BlockSpec((tm, tn), lambda i, j: (i, j)),
        out_shape=jax.ShapeDtypeStruct((t, n), BF16),
        scratch_shapes=[pltpu.VMEM((tm, D_MODEL), BF16)],
        compiler_params=_params(48, 2),
        name="modulated_matmul",
    )(*args)


POOL_PAD = 16
POOL_CHUNK = 256


def _pool_kernel(u_ref, lin_ref, scale_ref, o_ref, pad_ref, *, n):
    zeros = jnp.zeros((POOL_PAD, GROUP_DIM), F32)
    pad_ref[0:POOL_PAD, :] = zeros
    pad_ref[POOL_PAD + n:2 * POOL_PAD + n, :] = zeros
    for g, w in enumerate(POOL_WINDOWS):
        half = w // 2
        cols = slice(g * GROUP_DIM, (g + 1) * GROUP_DIM)
        pad_ref[POOL_PAD:POOL_PAD + n, :] = u_ref[:, cols].astype(F32)
        for r0 in range(0, n, POOL_CHUNK):
            tot = pad_ref[POOL_PAD + r0 - half:POOL_PAD + r0 - half + POOL_CHUNK, :]
            for d in range(-half + 1, half):
                tot = tot + pad_ref[POOL_PAD + r0 + d:POOL_PAD + r0 + d + POOL_CHUNK, :]
            t = r0 + lax.broadcasted_iota(jnp.int32, (POOL_CHUNK, 1), 0)
            cnt = (jnp.minimum(t + half, n) - jnp.maximum(t - half, 0)).astype(F32)
            pooled = tot / cnt - pad_ref[POOL_PAD + r0:POOL_PAD + r0 + POOL_CHUNK, :]
            y = jnp.dot(pooled.astype(BF16), lin_ref[g], preferred_element_type=F32) * scale_ref[g]
            o_ref[r0:r0 + POOL_CHUNK, cols] = y.astype(o_ref.dtype)


def _pool_mixer(u, lin, scale, n):
    t = u.shape[0]
    return pl.pallas_call(
        functools.partial(_pool_kernel, n=n),
        grid=(t // n,),
        in_specs=[
            pl.BlockSpec((n, POOL_WIDTH), lambda b: (b, 0)),
            pl.BlockSpec((N_GROUPS, GROUP_DIM, GROUP_DIM), lambda b: (0, 0, 0)),
            pl.BlockSpec((N_GROUPS, 1, GROUP_DIM), lambda b: (0, 0, 0)),
        ],
        out_specs=pl.BlockSpec((n, POOL_WIDTH), lambda b: (b, 0)),
        out_shape=jax.ShapeDtypeStruct((t, POOL_WIDTH), BF16),
        scratch_shapes=[pltpu.VMEM((n + 2 * POOL_PAD, GROUP_DIM), F32)],
        compiler_params=_params(32, 1),
        name="pool_mixer",
    )(u, lin, scale.reshape(N_GROUPS, 1, GROUP_DIM))


def _dft_tables(n):
    k = jnp.arange(n, dtype=jnp.int32)
    ang = ((k[:, None] * k[None, :]) % n).astype(F32) * (2.0 * math.pi / n)
    return jnp.cos(ang).astype(BF16), jnp.sin(ang).astype(BF16)


def _chan_dft_kernel(u_ref, cs_ref, a_ref, b_ref):
    for g in range(N_GROUPS):
        cols = slice(g * GROUP_DIM, (g + 1) * GROUP_DIM)
        ab = jnp.dot(u_ref[:, cols], cs_ref[...], preferred_element_type=F32)
        a_ref[:, cols] = ab[:, :GROUP_DIM].astype(a_ref.dtype)
        b_ref[:, cols] = ab[:, GROUP_DIM:].astype(b_ref.dtype)


def _chan_dft(u, cs):
    t = u.shape[0]
    tm = 1024
    width = N_GROUPS * GROUP_DIM
    out = jax.ShapeDtypeStruct((t, width), BF16)
    return pl.pallas_call(
        _chan_dft_kernel,
        grid=(t // tm,),
        in_specs=[
            pl.BlockSpec((tm, width), lambda i: (i, 1)),
            pl.BlockSpec((GROUP_DIM, 2 * GROUP_DIM), lambda i: (0, 0)),
        ],
        out_specs=[pl.BlockSpec((tm, width), lambda i: (i, 0))] * 2,
        out_shape=[out, out],
        compiler_params=_params(32, 1),
        name="channel_dft",
    )(u, cs)


def _seq_dft_kernel(c_ref, s_ref, a_ref, b_ref, lin_ref, o_ref, *, norm):
    z = (jnp.dot(c_ref[...], a_ref[...], preferred_element_type=F32)
         - jnp.dot(s_ref[...], b_ref[...], preferred_element_type=F32)) * norm
    zb = z.astype(BF16)
    for g in range(N_GROUPS):
        cols = slice(g * GROUP_DIM, (g + 1) * GROUP_DIM)
        o_ref[:, cols] = jnp.dot(zb[:, cols], lin_ref[g], preferred_element_type=F32).astype(o_ref.dtype)


def _seq_dft(cn, sn, a, b, lin, n):
    t = a.shape[0]
    tm = min(512, n)
    width = N_GROUPS * GROUP_DIM
    tiles = n // tm
    return pl.pallas_call(
        functools.partial(_seq_dft_kernel, norm=1.0 / math.sqrt(n * GROUP_DIM)),
        grid=(t // n, tiles),
        in_specs=[
            pl.BlockSpec((tm, n), lambda b_, i: (i, 0)),
            pl.BlockSpec((tm, n), lambda b_, i: (i, 0)),
            pl.BlockSpec((n, width), lambda b_, i: (b_, 0)),
            pl.BlockSpec((n, width), lambda b_, i: (b_, 0)),
            pl.BlockSpec((N_GROUPS, GROUP_DIM, GROUP_DIM), lambda b_, i: (0, 0, 0)),
        ],
        out_specs=pl.BlockSpec((tm, width), lambda b_, i: (b_ * tiles + i, 0)),
        out_shape=jax.ShapeDtypeStruct((t, width), BF16),
        compiler_params=_params(40, 2),
        name="sequence_dft",
    )(cn, sn, a, b, lin)


def _proj_ln_kernel(*refs, n_parts):
    y_refs = refs[:n_parts]
    w_refs = refs[n_parts:2 * n_parts]
    x_ref, g_ref, lng_ref, lnb_ref, o_ref = refs[2 * n_parts:]
    mx = jnp.dot(y_refs[0][...], w_refs[0][...], preferred_element_type=F32)
    for y_ref, w_ref in zip(y_refs[1:], w_refs[1:]):
        mx = mx + jnp.dot(y_ref[...], w_ref[...], preferred_element_type=F32)
    r = DEEPNORM_ALPHA * x_ref[...] + g_ref[0] * mx
    o_ref[...] = _layer_norm(r, lng_ref[...], lnb_ref[...])


def _proj_ln(ys, ws, x, mod, ln_g, ln_b, *, gate_k, seq_rows, const_row):
    t = x.shape[0]
    tm = 512
    n_parts = len(ys)
    in_specs = [pl.BlockSpec((tm, y.shape[1]), lambda i: (i, 0)) for y in ys]
    in_specs += [pl.BlockSpec(w.shape, lambda i: (0, 0)) for w in ws]
    in_specs += [
        pl.BlockSpec((tm, D_MODEL), lambda i: (i, 0)),
        _mod_spec(gate_k, tm, seq_rows, const_row),
        pl.BlockSpec((1, D_MODEL), lambda i: (0, 0)),
        pl.BlockSpec((1, D_MODEL), lambda i: (0, 0)),
    ]
    return pl.pallas_call(
        functools.partial(_proj_ln_kernel, n_parts=n_parts),
        grid=(t // tm,),
        in_specs=in_specs,
        out_specs=pl.BlockSpec((tm, D_MODEL), lambda i: (i, 0)),
        out_shape=jax.ShapeDtypeStruct((t, D_MODEL), F32),
        compiler_params=_params(48, 1),
        name="proj_layer_norm",
    )(*ys, *ws, x, mod, ln_g.reshape(1, D_MODEL), ln_b.reshape(1, D_MODEL))


def _ffn_kernel(x_ref, sh_ref, sc_ref, g_ref, wg_ref, wu_ref, wd_ref, lng_ref, lnb_ref, o_ref, h_ref, acc_ref):
    f = pl.program_id(1)

    @pl.when(f == 0)
    def _():
        h_ref[...] = (x_ref[...] * (1.0 + sc_ref[0]) + sh_ref[0]).astype(BF16)
        acc_ref[...] = jnp.zeros_like(acc_ref)

    h = h_ref[...]
    acts = []
    for c0 in range(0, wg_ref.shape[1], V7X_MXU_DIM):
        gate = jnp.dot(h, wg_ref[:, c0:c0 + V7X_MXU_DIM], preferred_element_type=F32)
        up = jnp.dot(h, wu_ref[:, c0:c0 + V7X_MXU_DIM], preferred_element_type=F32)
        acts.append((gate * jax.nn.sigmoid(gate) * up).astype(BF16))
    acc_ref[...] += jnp.dot(jnp.concatenate(acts, axis=1), wd_ref[...], preferred_element_type=F32)

    @pl.when(f == pl.num_programs(1) - 1)
    def _():
        r = DEEPNORM_ALPHA * x_ref[...] + g_ref[0] * acc_ref[...]
        o_ref[...] = _layer_norm(r, lng_ref[...], lnb_ref[...])


def _ffn_ln(x, mod, wg, wu, wd, ln_g, ln_b, *, seq_rows, const_row):
    t = x.shape[0]
    tm, tf = 512, 512
    row = lambda i, f: (i, 0)
    vec = pl.BlockSpec((1, D_MODEL), lambda i, f: (0, 0))
    return pl.pallas_call(
        _ffn_kernel,
        grid=(t // tm, FFN_HIDDEN // tf),
        in_specs=[
            pl.BlockSpec((tm, D_MODEL), row),
            _mod_spec(3, tm, seq_rows, const_row),
            _mod_spec(4, tm, seq_rows, const_row),
            _mod_spec(5, tm, seq_rows, const_row),
            pl.BlockSpec((D_MODEL, tf), lambda i, f: (0, f)),
            pl.BlockSpec((D_MODEL, tf), lambda i, f: (0, f)),
            pl.BlockSpec((tf, D_MODEL), lambda i, f: (f, 0)),
            vec,
            vec,
        ],
        out_specs=pl.BlockSpec((tm, D_MODEL), row),
        out_shape=jax.ShapeDtypeStruct((t, D_MODEL), F32),
        scratch_shapes=[pltpu.VMEM((tm, D_MODEL), BF16), pltpu.VMEM((tm, D_MODEL), F32)],
        compiler_params=_params(48, 2),
        name="swiglu_layer_norm",
    )(x, mod, mod, mod, wg, wu, wd, ln_g.reshape(1, D_MODEL), ln_b.reshape(1, D_MODEL))


def _rope_tables(n):
    rows = n // GRID_W
    row = jnp.repeat(jnp.arange(rows), GRID_W).astype(F32)
    col = jnp.tile(jnp.arange(GRID_W), rows).astype(F32)
    axis_dim = HEAD_DIM // 2
    inv_freq = ROPE_THETA ** (-jnp.arange(0, axis_dim, 2, dtype=F32) / axis_dim)
    ang_r = row[:, None] * inv_freq[None, :]
    ang_c = col[:, None] * inv_freq[None, :]
    cr, sr, cc, sc = jnp.cos(ang_r), jnp.sin(ang_r), jnp.cos(ang_c), jnp.sin(ang_c)
    zero = jnp.zeros_like(sr)
    cos = jnp.concatenate([cr, cr, cc, cc], axis=-1)
    sa = jnp.concatenate([-sr, zero, -sc, zero], axis=-1)
    sb = jnp.concatenate([zero, sr, zero, sc], axis=-1)
    return cos, sa, sb


def _attn_kernel(q_ref, kl_ref, vl_ref, kc_ref, vc_ref, lq1_ref, lk1_ref, lq2_ref, lk2_ref, g_ref,
                 o_ref, k_all, v_all, s_buf, mx_buf, w_buf, inv_buf, *, n, tq, lambda_init):
    k_all[0:CTX_LEN, :] = kc_ref[...]
    k_all[CTX_LEN:, :] = kl_ref[...]
    v_all[0:CTX_LEN, :] = vc_ref[...]
    v_all[CTX_LEN:, :] = vl_ref[...]
    lam = (jnp.exp(jnp.sum(lq1_ref[...] * lk1_ref[...], axis=-1, keepdims=True))
           - jnp.exp(jnp.sum(lq2_ref[...] * lk2_ref[...], axis=-1, keepdims=True)) + lambda_init)
    gain = g_ref[...] * (1.0 - lambda_init)
    nq = n // tq

    def rows(t):
        return pl.ds(pl.multiple_of(t * tq, tq), tq)

    def scores(t, slot):
        for m in range(2):
            lanes = slice(m * HEAD_DIM, (m + 1) * HEAD_DIM)
            s = lax.dot_general(q_ref[rows(t), lanes], k_all[:, lanes], (((1,), (1,)), ((), ())),
                                preferred_element_type=F32)
            s_buf[slot, m] = s
            mx_buf[slot, m] = jnp.max(s, axis=-1, keepdims=True)

    def softmax_diff(slot):
        sums = []
        for m in range(2):
            e = jnp.exp2(s_buf[slot, m] - mx_buf[slot, m])
            s_buf[slot, m] = e
            sums.append(jnp.sum(e, axis=-1, keepdims=True))
        ratio = lam * sums[0] / sums[1]
        w_buf[slot] = (s_buf[slot, 0] - s_buf[slot, 1] * ratio).astype(BF16)
        inv_buf[slot] = 1.0 / sums[0]

    def weighted_values(t, slot):
        o = jnp.dot(w_buf[slot], v_all[...], preferred_element_type=F32) * inv_buf[slot]
        o = o * lax.rsqrt(jnp.mean(o * o, axis=-1, keepdims=True) + RMS_EPS)
        o_ref[rows(t), :] = (o * gain).astype(o_ref.dtype)

    scores(0, 0)
    scores(1, 1)
    softmax_diff(0)

    def tile_pair(p, carry):
        t = 2 * p + 1
        scores(t + 1, 0)
        softmax_diff(1)
        weighted_values(t - 1, 0)
        scores(t + 2, 1)
        softmax_diff(0)
        weighted_values(t, 1)
        return carry

    lax.fori_loop(0, (nq - 2) // 2, tile_pair, 0)
    softmax_diff(1)
    weighted_values(nq - 2, 0)
    weighted_values(nq - 1, 1)


def _diff_attention(qk, v, kvc, lam_vecs, subln_g, n, lambda_init):
    t = qk.shape[0]
    tq = ATTN_TQ
    assert n % (2 * tq) == 0
    keys = CTX_LEN + n
    vec = pl.BlockSpec((1, HEAD_DIM), lambda b, h: (0, 0))
    return pl.pallas_call(
        functools.partial(_attn_kernel, n=n, tq=tq, lambda_init=lambda_init),
        grid=(t // n, N_HEADS),
        in_specs=[
            pl.BlockSpec((n, HEAD_V), lambda b, h: (b, h)),
            pl.BlockSpec((n, HEAD_V), lambda b, h: (b, N_HEADS + h)),
            pl.BlockSpec((n, HEAD_V), lambda b, h: (b, h)),
            pl.BlockSpec((CTX_LEN, HEAD_V), lambda b, h: (b, h)),
            pl.BlockSpec((CTX_LEN, HEAD_V), lambda b, h: (b, N_HEADS + h)),
            vec, vec, vec, vec,
            pl.BlockSpec((1, HEAD_V), lambda b, h: (0, 0)),
        ],
        out_specs=pl.BlockSpec((n, HEAD_V), lambda b, h: (b, h)),
        out_shape=jax.ShapeDtypeStruct((t, N_HEADS * HEAD_V), BF16),
        scratch_shapes=[
            pltpu.VMEM((keys, HEAD_V), BF16),
            pltpu.VMEM((keys, HEAD_V), BF16),
            pltpu.VMEM((2, 2, tq, keys), F32),
            pltpu.VMEM((2, 2, tq, 1), F32),
            pltpu.VMEM((2, tq, keys), BF16),
            pltpu.VMEM((2, tq, 1), F32),
        ],
        compiler_params=_params(48, 2),
        name="diff_attention",
    )(qk, qk, v, kvc, kvc, *[vv.reshape(1, HEAD_DIM) for vv in lam_vecs], subln_g.reshape(1, HEAD_V))


def _pool_fourier_mixer(h_src, mod, w_in, pool_lin, pool_scale, fourier_lin, chan_tab, seq_tabs, n, const_row):
    u = _modulated_matmul(h_src, mod, w_in, shift_k=0, scale_k=1, seq_rows=n, const_row=const_row)
    ya = _pool_mixer(u, pool_lin, pool_scale, n)
    a, b = _chan_dft(u, chan_tab)
    yb = _seq_dft(seq_tabs[0], seq_tabs[1], a, b, fourier_lin, n)
    return ya, yb


def kernel(x, c, ctx, c_ctx, w_mod, b_mod, ln1_g, ln1_b, ln2_g, ln2_b, ffn_w_gate, ffn_w_up, ffn_w_down,
           pf_w_in, pool_lin, pool_scale, fourier_lin, pf_w_out,
           da_w_in, da_lam_q1, da_lam_k1, da_lam_q2, da_lam_k2, da_subln_g, da_w_out):
    batch, n, _ = x.shape
    n_ctx = ctx.shape[1]
    ctx_row = batch
    xs = x.reshape(batch * n, D_MODEL)
    cs = ctx.reshape(batch * n_ctx, D_MODEL)
    c_all = jnp.zeros((MOD_ROWS, D_MODEL), F32).at[:batch].set(c).at[ctx_row].set(c_ctx)

    chan_cos, chan_sin = _dft_tables(GROUP_DIM)
    chan_tab = jnp.concatenate([chan_cos, chan_sin], axis=1)
    seq_tabs = {n: _dft_tables(n), n_ctx: _dft_tables(n_ctx)}
    rope_tabs = _rope_tables(n)

    for l in range(DEPTH):
        last = l == DEPTH - 1
        j = l // 2
        mod = _modulation(c_all, w_mod[l], b_mod[l]).reshape(MOD_ROWS * N_MOD, 1, D_MODEL)
        wg, wu, wd = ffn_w_gate[l].astype(BF16), ffn_w_up[l].astype(BF16), ffn_w_down[l].astype(BF16)
        streams = [(xs, n, None)] + ([] if last else [(cs, n_ctx, ctx_row)])
        if l % 2 == 0:
            w_in = pf_w_in[j].astype(BF16)
            w_out = pf_w_out[j].astype(BF16)
            p_lin, f_lin = pool_lin[j].astype(BF16), fourier_lin[j].astype(BF16)
            mixed = []
            for src, rows, const_row in streams:
                ya, yb = _pool_fourier_mixer(src, mod, w_in, p_lin, pool_scale[j], f_lin, chan_tab,
                                             seq_tabs[rows], rows, const_row)
                mixed.append(([ya, yb], [w_out[:POOL_WIDTH], w_out[POOL_WIDTH:]]))
        else:
            assert last, "context queries are only needed when a later layer consumes the context stream"
            lambda_init = 0.8 - 0.6 * math.exp(-0.3 * l)
            w_in = da_w_in[j].astype(BF16)
            w_out = da_w_out[j].astype(BF16)
            proj = functools.partial(_modulated_matmul, mod=mod, shift_k=0, scale_k=1)
            qk = proj(xs, w=w_in[:, :2 * QK_WIDTH], seq_rows=n, const_row=None, rope=rope_tabs,
                      q_cols=QK_WIDTH, q_scale=math.log2(math.e) / math.sqrt(HEAD_DIM))
            v = proj(xs, w=w_in[:, 2 * QK_WIDTH:], seq_rows=n, const_row=None)
            kvc = proj(cs, w=w_in[:, QK_WIDTH:], seq_rows=n_ctx, const_row=ctx_row)
            o_lat = _diff_attention(qk, v, kvc, (da_lam_q1[j], da_lam_k1[j], da_lam_q2[j], da_lam_k2[j]),
                                    da_subln_g[j], n, lambda_init)
            mixed = [([o_lat], [w_out])]
        new_streams = []
        for (src, rows, const_row), (ys, ws) in zip(streams, mixed):
            mid = _proj_ln(ys, ws, src, mod, ln1_g[l], ln1_b[l], gate_k=2, seq_rows=rows, const_row=const_row)
            new_streams.append(_ffn_ln(mid, mod, wg, wu, wd, ln2_g[l], ln2_b[l], seq_rows=rows,
                                       const_row=const_row))
        xs = new_streams[0]
        if not last:
            cs = new_streams[1]
    return xs.reshape(batch, n, D_MODEL)
```

```python
import functools
import math

import jax
import jax.numpy as jnp
from jax import lax
from jax.experimental import pallas as pl
from jax.experimental.pallas import tpu as pltpu

F32 = jnp.float32
BF16 = jnp.bfloat16

D_MODEL = 2048
DEPTH = 2
GRID_W = 64
CTX_LEN = 256
FFN_HIDDEN = 5632
POOL_WINDOWS = (2, 4, 8, 16)
POOL_WIDTH = 1024
GROUP_DIM = 256
N_GROUPS = 4
HEAD_DIM = 128
N_HEADS = 8
HEAD_V = 2 * HEAD_DIM
QK_WIDTH = 2048
ROPE_THETA = 10000.0
DEEPNORM_ALPHA = (2.0 * DEPTH) ** 0.25
LN_EPS = 1e-5
RMS_EPS = 1e-5
N_MOD = 6
MOD_ROWS = 24

V7X_MXU_DIM = 256
V7X_VMEM_BYTES = 64 * 1024 * 1024
MIB = 1024 * 1024

ATTN_TQ = 256
ROW_SUB = 256
LN_ROW_SUB = 128


def _params(vmem_mib, n_grid):
    assert vmem_mib * MIB < V7X_VMEM_BYTES
    return pltpu.CompilerParams(
        dimension_semantics=("arbitrary",) * n_grid,
        vmem_limit_bytes=vmem_mib * MIB,
    )


def _mod_spec(k, tm, seq_rows, const_row):
    if const_row is None:
        return pl.BlockSpec((1, 1, D_MODEL), lambda i, *_: ((i * tm) // seq_rows * N_MOD + k, 0, 0))
    return pl.BlockSpec((1, 1, D_MODEL), lambda i, *_: (const_row * N_MOD + k, 0, 0))


def _layer_norm(r, g, b):
    mu = jnp.mean(r, axis=-1, keepdims=True)
    d = r - mu
    var = jnp.mean(d * d, axis=-1, keepdims=True)
    return d * lax.rsqrt(var + LN_EPS) * g + b


def _mod_kernel(c_ref, w_ref, b_ref, o_ref):
    c = c_ref[...]
    s = c * jax.nn.sigmoid(c)
    o_ref[...] = jnp.dot(s, w_ref[...], preferred_element_type=F32) + b_ref[...]


def _modulation(c_all, w, b):
    tn = 1024
    n = w.shape[1]
    return pl.pallas_call(
        _mod_kernel,
        grid=(n // tn,),
        in_specs=[
            pl.BlockSpec((MOD_ROWS, D_MODEL), lambda j: (0, 0)),
            pl.BlockSpec((D_MODEL, tn), lambda j: (0, j)),
            pl.BlockSpec((1, tn), lambda j: (0, j)),
        ],
        out_specs=pl.BlockSpec((MOD_ROWS, tn), lambda j: (0, j)),
        out_shape=jax.ShapeDtypeStruct((MOD_ROWS, n), F32),
        compiler_params=_params(24, 1),
        name="modulation",
    )(c_all, w, b.reshape(1, n))


def _rope(acc, cos, sa, sb):
    outs = []
    for c0 in range(0, acc.shape[1], HEAD_DIM):
        xc = acc[:, c0:c0 + HEAD_DIM]
        up = pltpu.roll(xc, HEAD_DIM - 32, 1)
        dn = pltpu.roll(xc, 32, 1)
        outs.append(xc * cos + up * sa + dn * sb)
    return jnp.concatenate(outs, axis=1)


def _modmm_kernel(*refs, rope, q_tiles, q_scale):
    if rope:
        x_ref, sh_ref, sc_ref, w_ref, cos_ref, sa_ref, sb_ref, o_ref = refs
        fac = jnp.where(pl.program_id(1) < q_tiles, q_scale, 1.0).astype(F32)
    else:
        x_ref, sh_ref, sc_ref, w_ref, o_ref = refs
    scale1 = 1.0 + sc_ref[0]
    shift = sh_ref[0]
    for r0 in range(0, o_ref.shape[0], ROW_SUB):
        rs = slice(r0, r0 + ROW_SUB)
        h = (x_ref[rs, :] * scale1 + shift).astype(BF16)
        if rope:
            cos, sa, sb = cos_ref[rs, :] * fac, sa_ref[rs, :] * fac, sb_ref[rs, :] * fac
        for c0 in range(0, o_ref.shape[1], V7X_MXU_DIM):
            acc = jnp.dot(h, w_ref[:, c0:c0 + V7X_MXU_DIM], preferred_element_type=F32)
            if rope:
                acc = _rope(acc, cos, sa, sb)
            o_ref[rs, c0:c0 + V7X_MXU_DIM] = acc.astype(o_ref.dtype)


def _modulated_matmul(x, mod, w, *, shift_k, scale_k, seq_rows, const_row, rope=None, q_cols=0, q_scale=1.0):
    t, n = x.shape[0], w.shape[1]
    tm, tn = 1024, 1024
    in_specs = [
        pl.BlockSpec((tm, D_MODEL), lambda i, j: (i, 0)),
        _mod_spec(shift_k, tm, seq_rows, const_row),
        _mod_spec(scale_k, tm, seq_rows, const_row),
        pl.BlockSpec((D_MODEL, tn), lambda i, j: (0, j)),
    ]
    args = [x, mod, mod, w]
    if rope is not None:
        tiles_per_seq = seq_rows // tm
        tab_spec = pl.BlockSpec((tm, HEAD_DIM), lambda i, j: (i % tiles_per_seq, 0))
        in_specs += [tab_spec] * 3
        args += list(rope)
    return pl.pallas_call(
        functools.partial(_modmm_kernel, rope=rope is not None, q_tiles=q_cols // tn, q_scale=q_scale),
        grid=(t // tm, n // tn),
        in_specs=in_specs,
        out_specs=pl.BlockSpec((tm, tn), lambda i, j: (i, j)),
        out_shape=jax.ShapeDtypeStruct((t, n), BF16),
        compiler_params=_params(48, 2),
        name="modulated_matmul",
    )(*args)


POOL_CHUNK = 256
POOL_EDGE = 128
POOL_SPAN = POOL_CHUNK + 2 * POOL_EDGE


def _pool_bands():
    i = lax.broadcasted_iota(jnp.int32, (POOL_CHUNK, POOL_SPAN), 0)
    j = lax.broadcasted_iota(jnp.int32, (POOL_CHUNK, POOL_SPAN), 1)
    d = j - POOL_EDGE - i
    return jnp.stack([((d >= -(w // 2)) & (d < w - w // 2)).astype(BF16) for w in POOL_WINDOWS])


def _pool_kernel(u_ref, band_ref, lin_ref, scale_ref, o_ref, pad_ref, pooled_ref, *, n):
    zeros = jnp.zeros((POOL_EDGE, GROUP_DIM), BF16)
    for g, w in enumerate(POOL_WINDOWS):
        half = w // 2
        cols = slice(g * GROUP_DIM, (g + 1) * GROUP_DIM)
        pad_ref[g, 0:POOL_EDGE, :] = zeros
        pad_ref[g, POOL_EDGE + n:2 * POOL_EDGE + n, :] = zeros
        pad_ref[g, POOL_EDGE:POOL_EDGE + n, :] = u_ref[:, cols]
        for r0 in range(0, n, POOL_CHUNK):
            tot = jnp.dot(band_ref[g], pad_ref[g, r0:r0 + POOL_SPAN, :], preferred_element_type=F32)
            t = r0 + lax.broadcasted_iota(jnp.int32, (POOL_CHUNK, 1), 0)
            cnt = (jnp.minimum(t + half, n) - jnp.maximum(t - half, 0)).astype(F32)
            pooled = tot / cnt - u_ref[r0:r0 + POOL_CHUNK, cols].astype(F32)
            pooled_ref[g, r0:r0 + POOL_CHUNK, :] = pooled.astype(BF16)
        y = jnp.dot(pooled_ref[g], lin_ref[g], preferred_element_type=F32) * scale_ref[g]
        o_ref[:, cols] = y.astype(o_ref.dtype)


def _pool_mixer(u, bands, lin, scale, n):
    t = u.shape[0]
    return pl.pallas_call(
        functools.partial(_pool_kernel, n=n),
        grid=(t // n,),
        in_specs=[
            pl.BlockSpec((n, POOL_WIDTH), lambda b: (b, 0)),
            pl.BlockSpec((N_GROUPS, POOL_CHUNK, POOL_SPAN), lambda b: (0, 0, 0)),
            pl.BlockSpec((N_GROUPS, GROUP_DIM, GROUP_DIM), lambda b: (0, 0, 0)),
            pl.BlockSpec((N_GROUPS, 1, GROUP_DIM), lambda b: (0, 0, 0)),
        ],
        out_specs=pl.BlockSpec((n, POOL_WIDTH), lambda b: (b, 0)),
        out_shape=jax.ShapeDtypeStruct((t, POOL_WIDTH), BF16),
        scratch_shapes=[pltpu.VMEM((N_GROUPS, n + 2 * POOL_EDGE, GROUP_DIM), BF16),
                        pltpu.VMEM((N_GROUPS, n, GROUP_DIM), BF16)],
        compiler_params=_params(32, 1),
        name="pool_mixer",
    )(u, bands, lin, scale.reshape(N_GROUPS, 1, GROUP_DIM))


def _dft_tables(n):
    k = jnp.arange(n, dtype=jnp.int32)
    ang = ((k[:, None] * k[None, :]) % n).astype(F32) * (2.0 * math.pi / n)
    return jnp.cos(ang).astype(BF16), jnp.sin(ang).astype(BF16)


def _chan_dft_kernel(u_ref, cs_ref, a_ref, b_ref):
    for g in range(N_GROUPS):
        cols = slice(g * GROUP_DIM, (g + 1) * GROUP_DIM)
        ab = jnp.dot(u_ref[:, cols], cs_ref[...], preferred_element_type=F32)
        a_ref[:, cols] = ab[:, :GROUP_DIM].astype(a_ref.dtype)
        b_ref[:, cols] = ab[:, GROUP_DIM:].astype(b_ref.dtype)


def _chan_dft(u, cs):
    t = u.shape[0]
    tm = 1024
    width = N_GROUPS * GROUP_DIM
    out = jax.ShapeDtypeStruct((t, width), BF16)
    return pl.pallas_call(
        _chan_dft_kernel,
        grid=(t // tm,),
        in_specs=[
            pl.BlockSpec((tm, width), lambda i: (i, 1)),
            pl.BlockSpec((GROUP_DIM, 2 * GROUP_DIM), lambda i: (0, 0)),
        ],
        out_specs=[pl.BlockSpec((tm, width), lambda i: (i, 0))] * 2,
        out_shape=[out, out],
        compiler_params=_params(32, 1),
        name="channel_dft",
    )(u, cs)


def _seq_dft_kernel(c_ref, s_ref, a_ref, b_ref, lin_ref, o_ref, *, norm):
    z = (jnp.dot(c_ref[...], a_ref[...], preferred_element_type=F32)
         - jnp.dot(s_ref[...], b_ref[...], preferred_element_type=F32)) * norm
    zb = z.astype(BF16)
    for g in range(N_GROUPS):
        cols = slice(g * GROUP_DIM, (g + 1) * GROUP_DIM)
        o_ref[:, cols] = jnp.dot(zb[:, cols], lin_ref[g], preferred_element_type=F32).astype(o_ref.dtype)


def _seq_dft(cn, sn, a, b, lin, n):
    t = a.shape[0]
    tm = min(512, n)
    width = N_GROUPS * GROUP_DIM
    tiles = n // tm
    return pl.pallas_call(
        functools.partial(_seq_dft_kernel, norm=1.0 / math.sqrt(n * GROUP_DIM)),
        grid=(t // n, tiles),
        in_specs=[
            pl.BlockSpec((tm, n), lambda b_, i: (i, 0)),
            pl.BlockSpec((tm, n), lambda b_, i: (i, 0)),
            pl.BlockSpec((n, width), lambda b_, i: (b_, 0)),
            pl.BlockSpec((n, width), lambda b_, i: (b_, 0)),
            pl.BlockSpec((N_GROUPS, GROUP_DIM, GROUP_DIM), lambda b_, i: (0, 0, 0)),
        ],
        out_specs=pl.BlockSpec((tm, width), lambda b_, i: (b_ * tiles + i, 0)),
        out_shape=jax.ShapeDtypeStruct((t, width), BF16),
        compiler_params=_params(40, 2),
        name="sequence_dft",
    )(cn, sn, a, b, lin)


def _proj_ln_kernel(*refs, n_parts):
    y_refs = refs[:n_parts]
    w_refs = refs[n_parts:2 * n_parts]
    x_ref, g_ref, lng_ref, lnb_ref, o_ref = refs[2 * n_parts:]
    gate, ln_g, ln_b = g_ref[0], lng_ref[...], lnb_ref[...]
    for r0 in range(0, o_ref.shape[0], LN_ROW_SUB):
        rs = slice(r0, r0 + LN_ROW_SUB)
        mx = jnp.dot(y_refs[0][rs, :], w_refs[0][...], preferred_element_type=F32)
        for y_ref, w_ref in zip(y_refs[1:], w_refs[1:]):
            mx = mx + jnp.dot(y_ref[rs, :], w_ref[...], preferred_element_type=F32)
        r = DEEPNORM_ALPHA * x_ref[rs, :] + gate * mx
        o_ref[rs, :] = _layer_norm(r, ln_g, ln_b)


def _proj_ln(ys, ws, x, mod, ln_g, ln_b, *, gate_k, seq_rows, const_row):
    t = x.shape[0]
    tm = 512
    n_parts = len(ys)
    in_specs = [pl.BlockSpec((tm, y.shape[1]), lambda i: (i, 0)) for y in ys]
    in_specs += [pl.BlockSpec(w.shape, lambda i: (0, 0)) for w in ws]
    in_specs += [
        pl.BlockSpec((tm, D_MODEL), lambda i: (i, 0)),
        _mod_spec(gate_k, tm, seq_rows, const_row),
        pl.BlockSpec((1, D_MODEL), lambda i: (0, 0)),
        pl.BlockSpec((1, D_MODEL), lambda i: (0, 0)),
    ]
    return pl.pallas_call(
        functools.partial(_proj_ln_kernel, n_parts=n_parts),
        grid=(t // tm,),
        in_specs=in_specs,
        out_specs=pl.BlockSpec((tm, D_MODEL), lambda i: (i, 0)),
        out_shape=jax.ShapeDtypeStruct((t, D_MODEL), F32),
        compiler_params=_params(48, 1),
        name="proj_layer_norm",
    )(*ys, *ws, x, mod, ln_g.reshape(1, D_MODEL), ln_b.reshape(1, D_MODEL))


def _ffn_kernel(x_ref, sh_ref, sc_ref, g_ref, wg_ref, wu_ref, wd_ref, lng_ref, lnb_ref, o_ref, acc_ref):
    f = pl.program_id(1)
    last = pl.num_programs(1) - 1
    tm = o_ref.shape[0]
    scale1 = 1.0 + sc_ref[0]
    shift = sh_ref[0]

    def hidden_chunk(rs):
        h = (x_ref[rs, :] * scale1 + shift).astype(BF16)
        acts = []
        for c0 in range(0, wg_ref.shape[1], V7X_MXU_DIM):
            gate = jnp.dot(h, wg_ref[:, c0:c0 + V7X_MXU_DIM], preferred_element_type=F32)
            up = jnp.dot(h, wu_ref[:, c0:c0 + V7X_MXU_DIM], preferred_element_type=F32)
            acts.append((gate * jax.nn.sigmoid(gate) * up).astype(BF16))
        return jnp.dot(jnp.concatenate(acts, axis=1), wd_ref[...], preferred_element_type=F32)

    @pl.when(f == 0)
    def _():
        for r0 in range(0, tm, ROW_SUB):
            rs = slice(r0, r0 + ROW_SUB)
            acc_ref[rs, :] = hidden_chunk(rs)

    @pl.when((f > 0) & (f < last))
    def _():
        for r0 in range(0, tm, ROW_SUB):
            rs = slice(r0, r0 + ROW_SUB)
            acc_ref[rs, :] += hidden_chunk(rs)

    @pl.when(f == last)
    def _():
        gate, ln_g, ln_b = g_ref[0], lng_ref[...], lnb_ref[...]
        for r0 in range(0, tm, ROW_SUB):
            rs = slice(r0, r0 + ROW_SUB)
            r = DEEPNORM_ALPHA * x_ref[rs, :] + gate * (acc_ref[rs, :] + hidden_chunk(rs))
            o_ref[rs, :] = _layer_norm(r, ln_g, ln_b)


def _ffn_ln(x, mod, wg, wu, wd, ln_g, ln_b, *, seq_rows, const_row):
    t = x.shape[0]
    tm, tf = 512, 512
    row = lambda i, f: (i, 0)
    vec = pl.BlockSpec((1, D_MODEL), lambda i, f: (0, 0))
    return pl.pallas_call(
        _ffn_kernel,
        grid=(t // tm, FFN_HIDDEN // tf),
        in_specs=[
            pl.BlockSpec((tm, D_MODEL), row),
            _mod_spec(3, tm, seq_rows, const_row),
            _mod_spec(4, tm, seq_rows, const_row),
            _mod_spec(5, tm, seq_rows, const_row),
            pl.BlockSpec((D_MODEL, tf), lambda i, f: (0, f)),
            pl.BlockSpec((D_MODEL, tf), lambda i, f: (0, f)),
            pl.BlockSpec((tf, D_MODEL), lambda i, f: (f, 0)),
            vec,
            vec,
        ],
        out_specs=pl.BlockSpec((tm, D_MODEL), row),
        out_shape=jax.ShapeDtypeStruct((t, D_MODEL), F32),
        scratch_shapes=[pltpu.VMEM((tm, D_MODEL), F32)],
        compiler_params=_params(48, 2),
        name="swiglu_layer_norm",
    )(x, mod, mod, mod, wg, wu, wd, ln_g.reshape(1, D_MODEL), ln_b.reshape(1, D_MODEL))


def _rope_tables(n):
    rows = n // GRID_W
    row = jnp.repeat(jnp.arange(rows), GRID_W).astype(F32)
    col = jnp.tile(jnp.arange(GRID_W), rows).astype(F32)
    axis_dim = HEAD_DIM // 2
    inv_freq = ROPE_THETA ** (-jnp.arange(0, axis_dim, 2, dtype=F32) / axis_dim)
    ang_r = row[:, None] * inv_freq[None, :]
    ang_c = col[:, None] * inv_freq[None, :]
    cr, sr, cc, sc = jnp.cos(ang_r), jnp.sin(ang_r), jnp.cos(ang_c), jnp.sin(ang_c)
    zero = jnp.zeros_like(sr)
    cos = jnp.concatenate([cr, cr, cc, cc], axis=-1)
    sa = jnp.concatenate([-sr, zero, -sc, zero], axis=-1)
    sb = jnp.concatenate([zero, sr, zero, sc], axis=-1)
    return cos, sa, sb


def _attn_kernel(q_ref, kl_ref, vl_ref, kc_ref, vc_ref, lq1_ref, lk1_ref, lq2_ref, lk2_ref, g_ref,
                 o_ref, k_all, v_all, s_buf, mx_buf, w_buf, inv_buf, *, n, tq, lambda_init):
    k_all[0:CTX_LEN, :] = kc_ref[...]
    k_all[CTX_LEN:, :] = kl_ref[...]
    v_all[0:CTX_LEN, :] = vc_ref[...]
    v_all[CTX_LEN:, :] = vl_ref[...]
    lam = (jnp.exp(jnp.sum(lq1_ref[...] * lk1_ref[...], axis=-1, keepdims=True))
           - jnp.exp(jnp.sum(lq2_ref[...] * lk2_ref[...], axis=-1, keepdims=True)) + lambda_init)
    gain = g_ref[...] * (1.0 - lambda_init)
    nq = n // tq

    def rows(t):
        return pl.ds(pl.multiple_of(t * tq, tq), tq)

    def scores(t, slot):
        for m in range(2):
            lanes = slice(m * HEAD_DIM, (m + 1) * HEAD_DIM)
            s = lax.dot_general(q_ref[rows(t), lanes], k_all[:, lanes], (((1,), (1,)), ((), ())),
                                preferred_element_type=F32)
            s_buf[slot, m] = s
            mx_buf[slot, m] = jnp.max(s, axis=-1, keepdims=True)

    def softmax_diff(slot):
        sums = []
        for m in range(2):
            e = jnp.exp2(s_buf[slot, m] - mx_buf[slot, m])
            s_buf[slot, m] = e
            sums.append(jnp.sum(e, axis=-1, keepdims=True))
        ratio = lam * sums[0] / sums[1]
        w_buf[slot] = (s_buf[slot, 0] - s_buf[slot, 1] * ratio).astype(BF16)
        inv_buf[slot] = 1.0 / sums[0]

    def weighted_values(t, slot):
        o = jnp.dot(w_buf[slot], v_all[...], preferred_element_type=F32) * inv_buf[slot]
        o = o * lax.rsqrt(jnp.mean(o * o, axis=-1, keepdims=True) + RMS_EPS)
        o_ref[rows(t), :] = (o * gain).astype(o_ref.dtype)

    scores(0, 0)
    scores(1, 1)
    softmax_diff(0)

    def tile_pair(p, carry):
        t = 2 * p + 1
        scores(t + 1, 0)
        softmax_diff(1)
        weighted_values(t - 1, 0)
        scores(t + 2, 1)
        softmax_diff(0)
        weighted_values(t, 1)
        return carry

    lax.fori_loop(0, (nq - 2) // 2, tile_pair, 0)
    softmax_diff(1)
    weighted_values(nq - 2, 0)
    weighted_values(nq - 1, 1)


def _diff_attention(qk, v, kvc, lam_vecs, subln_g, n, lambda_init):
    t = qk.shape[0]
    tq = ATTN_TQ
    assert n % (2 * tq) == 0
    keys = CTX_LEN + n
    vec = pl.BlockSpec((1, HEAD_DIM), lambda b, h: (0, 0))
    return pl.pallas_call(
        functools.partial(_attn_kernel, n=n, tq=tq, lambda_init=lambda_init),
        grid=(t // n, N_HEADS),
        in_specs=[
            pl.BlockSpec((n, HEAD_V), lambda b, h: (b, h)),
            pl.BlockSpec((n, HEAD_V), lambda b, h: (b, N_HEADS + h)),
            pl.BlockSpec((n, HEAD_V), lambda b, h: (b, h)),
            pl.BlockSpec((CTX_LEN, HEAD_V), lambda b, h: (b, h)),
            pl.BlockSpec((CTX_LEN, HEAD_V), lambda b, h: (b, N_HEADS + h)),
            vec, vec, vec, vec,
            pl.BlockSpec((1, HEAD_V), lambda b, h: (0, 0)),
        ],
        out_specs=pl.BlockSpec((n, HEAD_V), lambda b, h: (b, h)),
        out_shape=jax.ShapeDtypeStruct((t, N_HEADS * HEAD_V), BF16),
        scratch_shapes=[
            pltpu.VMEM((keys, HEAD_V), BF16),
            pltpu.VMEM((keys, HEAD_V), BF16),
            pltpu.VMEM((2, 2, tq, keys), F32),
            pltpu.VMEM((2, 2, tq, 1), F32),
            pltpu.VMEM((2, tq, keys), BF16),
            pltpu.VMEM((2, tq, 1), F32),
        ],
        compiler_params=_params(48, 2),
        name="diff_attention",
    )(qk, qk, v, kvc, kvc, *[vv.reshape(1, HEAD_DIM) for vv in lam_vecs], subln_g.reshape(1, HEAD_V))


def _pool_fourier_mixer(h_src, mod, w_in, bands, pool_lin, pool_scale, fourier_lin, chan_tab, seq_tabs, n,
                        const_row):
    u = _modulated_matmul(h_src, mod, w_in, shift_k=0, scale_k=1, seq_rows=n, const_row=const_row)
    ya = _pool_mixer(u, bands, pool_lin, pool_scale, n)
    a, b = _chan_dft(u, chan_tab)
    yb = _seq_dft(seq_tabs[0], seq_tabs[1], a, b, fourier_lin, n)
    return ya, yb


def kernel(x, c, ctx, c_ctx, w_mod, b_mod, ln1_g, ln1_b, ln2_g, ln2_b, ffn_w_gate, ffn_w_up, ffn_w_down,
           pf_w_in, pool_lin, pool_scale, fourier_lin, pf_w_out,
           da_w_in, da_lam_q1, da_lam_k1, da_lam_q2, da_lam_k2, da_subln_g, da_w_out):
    batch, n, _ = x.shape
    n_ctx = ctx.shape[1]
    ctx_row = batch
    xs = x.reshape(batch * n, D_MODEL)
    cs = ctx.reshape(batch * n_ctx, D_MODEL)
    c_all = jnp.zeros((MOD_ROWS, D_MODEL), F32).at[:batch].set(c).at[ctx_row].set(c_ctx)

    chan_cos, chan_sin = _dft_tables(GROUP_DIM)
    chan_tab = jnp.concatenate([chan_cos, chan_sin], axis=1)
    seq_tabs = {n: _dft_tables(n), n_ctx: _dft_tables(n_ctx)}
    rope_tabs = _rope_tables(n)
    bands = _pool_bands()

    for l in range(DEPTH):
        last = l == DEPTH - 1
        j = l // 2
        mod = _modulation(c_all, w_mod[l], b_mod[l]).reshape(MOD_ROWS * N_MOD, 1, D_MODEL)
        wg, wu, wd = ffn_w_gate[l].astype(BF16), ffn_w_up[l].astype(BF16), ffn_w_down[l].astype(BF16)
        streams = [(xs, n, None)] + ([] if last else [(cs, n_ctx, ctx_row)])
        if l % 2 == 0:
            w_in = pf_w_in[j].astype(BF16)
            w_out = pf_w_out[j].astype(BF16)
            p_lin, f_lin = pool_lin[j].astype(BF16), fourier_lin[j].astype(BF16)
            mixed = []
            for src, rows, const_row in streams:
                ya, yb = _pool_fourier_mixer(src, mod, w_in, bands, p_lin, pool_scale[j], f_lin, chan_tab,
                                             seq_tabs[rows], rows, const_row)
                mixed.append(([ya, yb], [w_out[:POOL_WIDTH], w_out[POOL_WIDTH:]]))
        else:
            assert last, "context queries are only needed when a later layer consumes the context stream"
            lambda_init = 0.8 - 0.6 * math.exp(-0.3 * l)
            w_in = da_w_in[j].astype(BF16)
            w_out = da_w_out[j].astype(BF16)
            proj = functools.partial(_modulated_matmul, mod=mod, shift_k=0, scale_k=1)
            qk = proj(xs, w=w_in[:, :2 * QK_WIDTH], seq_rows=n, const_row=None, rope=rope_tabs,
                      q_cols=QK_WIDTH, q_scale=math.log2(math.e) / math.sqrt(HEAD_DIM))
            v = proj(xs, w=w_in[:, 2 * QK_WIDTH:], seq_rows=n, const_row=None)
            kvc = proj(cs, w=w_in[:, QK_WIDTH:], seq_rows=n_ctx, const_row=ctx_row)
            o_lat = _diff_attention(qk, v, kvc, (da_lam_q1[j], da_lam_k1[j], da_lam_q2[j], da_lam_k2[j]),
                                    da_subln_g[j], n, lambda_init)
            mixed = [([o_lat], [w_out])]
        new_streams = []
        for (src, rows, const_row), (ys, ws) in zip(streams, mixed):
            mid = _proj_ln(ys, ws, src, mod, ln1_g[l], ln1_b[l], gate_k=2, seq_rows=rows, const_row=const_row)
            new_streams.append(_ffn_ln(mid, mod, wg, wu, wd, ln2_g[l], ln2_b[l], seq_rows=rows,
                                       const_row=const_row))
        xs = new_streams[0]
        if not last:
            cs = new_streams[1]
    return xs.reshape(batch, n, D_MODEL)
```

```python
import functools
import math

import jax
import jax.numpy as jnp
from jax import lax
from jax.experimental import pallas as pl
from jax.experimental.pallas import tpu as pltpu

F32 = jnp.float32
BF16 = jnp.bfloat16

D_MODEL = 2048
DEPTH = 2
GRID_W = 64
CTX_LEN = 256
FFN_HIDDEN = 5632
POOL_WINDOWS = (2, 4, 8, 16)
POOL_WIDTH = 1024
GROUP_DIM = 256
N_GROUPS = 4
HEAD_DIM = 128
N_HEADS = 8
HEAD_V = 2 * HEAD_DIM
QK_WIDTH = 2048
ROPE_THETA = 10000.0
DEEPNORM_ALPHA = (2.0 * DEPTH) ** 0.25
LN_EPS = 1e-5
RMS_EPS = 1e-5
N_MOD = 6
MOD_ROWS = 24

V7X_MXU_DIM = 256
V7X_VMEM_BYTES = 64 * 1024 * 1024
MIB = 1024 * 1024

ATTN_TQ = 256
ROW_SUB = 256
LN_ROW_SUB = 128


def _params(vmem_mib, n_grid):
    assert vmem_mib * MIB < V7X_VMEM_BYTES
    return pltpu.CompilerParams(
        dimension_semantics=("arbitrary",) * n_grid,
        vmem_limit_bytes=vmem_mib * MIB,
    )


def _mod_spec(k, tm, seq_rows, const_row):
    if const_row is None:
        return pl.BlockSpec((1, 1, D_MODEL), lambda i, *_: ((i * tm) // seq_rows * N_MOD + k, 0, 0))
    return pl.BlockSpec((1, 1, D_MODEL), lambda i, *_: (const_row * N_MOD + k, 0, 0))


def _layer_norm(r, g, b):
    mu = jnp.mean(r, axis=-1, keepdims=True)
    d = r - mu
    var = jnp.mean(d * d, axis=-1, keepdims=True)
    return d * lax.rsqrt(var + LN_EPS) * g + b


def _mod_kernel(c_ref, w_ref, b_ref, o_ref):
    c = c_ref[...]
    s = c * jax.nn.sigmoid(c)
    o_ref[...] = jnp.dot(s, w_ref[...], preferred_element_type=F32) + b_ref[...]


def _modulation(c_all, w, b):
    tn = 1024
    n = w.shape[1]
    return pl.pallas_call(
        _mod_kernel,
        grid=(n // tn,),
        in_specs=[
            pl.BlockSpec((MOD_ROWS, D_MODEL), lambda j: (0, 0)),
            pl.BlockSpec((D_MODEL, tn), lambda j: (0, j)),
            pl.BlockSpec((1, tn), lambda j: (0, j)),
        ],
        out_specs=pl.BlockSpec((MOD_ROWS, tn), lambda j: (0, j)),
        out_shape=jax.ShapeDtypeStruct((MOD_ROWS, n), F32),
        compiler_params=_params(24, 1),
        name="modulation",
    )(c_all, w, b.reshape(1, n))


def _rope(acc, cos, sa, sb):
    outs = []
    for c0 in range(0, acc.shape[1], HEAD_DIM):
        xc = acc[:, c0:c0 + HEAD_DIM]
        up = pltpu.roll(xc, HEAD_DIM - 32, 1)
        dn = pltpu.roll(xc, 32, 1)
        outs.append(xc * cos + up * sa + dn * sb)
    return jnp.concatenate(outs, axis=1)


def _modmm_kernel(*refs, rope, q_tiles, q_scale):
    if rope:
        x_ref, sh_ref, sc_ref, w_ref, cos_ref, sa_ref, sb_ref, o_ref = refs
        fac = jnp.where(pl.program_id(1) < q_tiles, q_scale, 1.0).astype(F32)
    else:
        x_ref, sh_ref, sc_ref, w_ref, o_ref = refs
    scale1 = 1.0 + sc_ref[0]
    shift = sh_ref[0]
    for r0 in range(0, o_ref.shape[0], ROW_SUB):
        rs = slice(r0, r0 + ROW_SUB)
        h = (x_ref[rs, :] * scale1 + shift).astype(BF16)
        if rope:
            cos, sa, sb = cos_ref[rs, :] * fac, sa_ref[rs, :] * fac, sb_ref[rs, :] * fac
        for c0 in range(0, o_ref.shape[1], V7X_MXU_DIM):
            acc = jnp.dot(h, w_ref[:, c0:c0 + V7X_MXU_DIM], preferred_element_type=F32)
            if rope:
                acc = _rope(acc, cos, sa, sb)
            o_ref[rs, c0:c0 + V7X_MXU_DIM] = acc.astype(o_ref.dtype)


def _modulated_matmul(x, mod, w, *, shift_k, scale_k, seq_rows, const_row, rope=None, q_cols=0, q_scale=1.0):
    t, n = x.shape[0], w.shape[1]
    tm, tn = 1024, min(n, 2048)
    in_specs = [
        pl.BlockSpec((tm, D_MODEL), lambda i, j: (i, 0)),
        _mod_spec(shift_k, tm, seq_rows, const_row),
        _mod_spec(scale_k, tm, seq_rows, const_row),
        pl.BlockSpec((D_MODEL, tn), lambda i, j: (0, j)),
    ]
    args = [x, mod, mod, w]
    if rope is not None:
        tiles_per_seq = seq_rows // tm
        tab_spec = pl.BlockSpec((tm, HEAD_DIM), lambda i, j: (i % tiles_per_seq, 0))
        in_specs += [tab_spec] * 3
        args += list(rope)
    return pl.pallas_call(
        functools.partial(_modmm_kernel, rope=rope is not None, q_tiles=q_cols // tn, q_scale=q_scale),
        grid=(t // tm, n // tn),
        in_specs=in_specs,
        out_specs=pl.BlockSpec((tm, tn), lambda i, j: (i, j)),
        out_shape=jax.ShapeDtypeStruct((t, n), BF16),
        compiler_params=_params(56, 2),
        name="modulated_matmul",
    )(*args)


POOL_CHUNK = 256
POOL_EDGE = 128
POOL_SPAN = POOL_CHUNK + 2 * POOL_EDGE


def _pool_bands():
    i = lax.broadcasted_iota(jnp.int32, (POOL_CHUNK, POOL_SPAN), 0)
    j = lax.broadcasted_iota(jnp.int32, (POOL_CHUNK, POOL_SPAN), 1)
    d = j - POOL_EDGE - i
    return jnp.stack([((d >= -(w // 2)) & (d < w - w // 2)).astype(BF16) for w in POOL_WINDOWS])


def _pool_kernel(u_ref, band_ref, lin_ref, scale_ref, o_ref, pad_ref, pooled_ref, *, n):
    zeros = jnp.zeros((POOL_EDGE, GROUP_DIM), BF16)
    for g, w in enumerate(POOL_WINDOWS):
        half = w // 2
        cols = slice(g * GROUP_DIM, (g + 1) * GROUP_DIM)
        pad_ref[g, 0:POOL_EDGE, :] = zeros
        pad_ref[g, POOL_EDGE + n:2 * POOL_EDGE + n, :] = zeros
        pad_ref[g, POOL_EDGE:POOL_EDGE + n, :] = u_ref[:, cols]
        for r0 in range(0, n, POOL_CHUNK):
            tot = jnp.dot(band_ref[g], pad_ref[g, r0:r0 + POOL_SPAN, :], preferred_element_type=F32)
            t = r0 + lax.broadcasted_iota(jnp.int32, (POOL_CHUNK, 1), 0)
            cnt = (jnp.minimum(t + half, n) - jnp.maximum(t - half, 0)).astype(F32)
            pooled = tot / cnt - u_ref[r0:r0 + POOL_CHUNK, cols].astype(F32)
            pooled_ref[g, r0:r0 + POOL_CHUNK, :] = pooled.astype(BF16)
        y = jnp.dot(pooled_ref[g], lin_ref[g], preferred_element_type=F32) * scale_ref[g]
        o_ref[:, cols] = y.astype(o_ref.dtype)


def _pool_mixer(u, bands, lin, scale, n):
    t = u.shape[0]
    return pl.pallas_call(
        functools.partial(_pool_kernel, n=n),
        grid=(t // n,),
        in_specs=[
            pl.BlockSpec((n, POOL_WIDTH), lambda b: (b, 0)),
            pl.BlockSpec((N_GROUPS, POOL_CHUNK, POOL_SPAN), lambda b: (0, 0, 0)),
            pl.BlockSpec((N_GROUPS, GROUP_DIM, GROUP_DIM), lambda b: (0, 0, 0)),
            pl.BlockSpec((N_GROUPS, 1, GROUP_DIM), lambda b: (0, 0, 0)),
        ],
        out_specs=pl.BlockSpec((n, POOL_WIDTH), lambda b: (b, 0)),
        out_shape=jax.ShapeDtypeStruct((t, POOL_WIDTH), BF16),
        scratch_shapes=[pltpu.VMEM((N_GROUPS, n + 2 * POOL_EDGE, GROUP_DIM), BF16),
                        pltpu.VMEM((N_GROUPS, n, GROUP_DIM), BF16)],
        compiler_params=_params(32, 1),
        name="pool_mixer",
    )(u, bands, lin, scale.reshape(N_GROUPS, 1, GROUP_DIM))


def _dft_tables(n):
    k = jnp.arange(n, dtype=jnp.int32)
    ang = ((k[:, None] * k[None, :]) % n).astype(F32) * (2.0 * math.pi / n)
    return jnp.cos(ang).astype(BF16), jnp.sin(ang).astype(BF16)


def _chan_dft_kernel(u_ref, cs_ref, a_ref, b_ref):
    for g in range(N_GROUPS):
        cols = slice(g * GROUP_DIM, (g + 1) * GROUP_DIM)
        ab = jnp.dot(u_ref[:, cols], cs_ref[...], preferred_element_type=F32)
        a_ref[:, cols] = ab[:, :GROUP_DIM].astype(a_ref.dtype)
        b_ref[:, cols] = ab[:, GROUP_DIM:].astype(b_ref.dtype)


def _chan_dft(u, cs):
    t = u.shape[0]
    tm = 1024
    width = N_GROUPS * GROUP_DIM
    out = jax.ShapeDtypeStruct((t, width), BF16)
    return pl.pallas_call(
        _chan_dft_kernel,
        grid=(t // tm,),
        in_specs=[
            pl.BlockSpec((tm, width), lambda i: (i, 1)),
            pl.BlockSpec((GROUP_DIM, 2 * GROUP_DIM), lambda i: (0, 0)),
        ],
        out_specs=[pl.BlockSpec((tm, width), lambda i: (i, 0))] * 2,
        out_shape=[out, out],
        compiler_params=_params(32, 1),
        name="channel_dft",
    )(u, cs)


def _seq_dft_kernel(c_ref, s_ref, a_ref, b_ref, lin_ref, o_ref, *, norm):
    z = (jnp.dot(c_ref[...], a_ref[...], preferred_element_type=F32)
         - jnp.dot(s_ref[...], b_ref[...], preferred_element_type=F32)) * norm
    zb = z.astype(BF16)
    for g in range(N_GROUPS):
        cols = slice(g * GROUP_DIM, (g + 1) * GROUP_DIM)
        o_ref[:, cols] = jnp.dot(zb[:, cols], lin_ref[g], preferred_element_type=F32).astype(o_ref.dtype)


def _seq_dft(cn, sn, a, b, lin, n):
    t = a.shape[0]
    tm = min(512, n)
    width = N_GROUPS * GROUP_DIM
    tiles = n // tm
    return pl.pallas_call(
        functools.partial(_seq_dft_kernel, norm=1.0 / math.sqrt(n * GROUP_DIM)),
        grid=(t // n, tiles),
        in_specs=[
            pl.BlockSpec((tm, n), lambda b_, i: (i, 0)),
            pl.BlockSpec((tm, n), lambda b_, i: (i, 0)),
            pl.BlockSpec((n, width), lambda b_, i: (b_, 0)),
            pl.BlockSpec((n, width), lambda b_, i: (b_, 0)),
            pl.BlockSpec((N_GROUPS, GROUP_DIM, GROUP_DIM), lambda b_, i: (0, 0, 0)),
        ],
        out_specs=pl.BlockSpec((tm, width), lambda b_, i: (b_ * tiles + i, 0)),
        out_shape=jax.ShapeDtypeStruct((t, width), BF16),
        compiler_params=_params(40, 2),
        name="sequence_dft",
    )(cn, sn, a, b, lin)


def _proj_ln_kernel(*refs, n_parts):
    y_refs = refs[:n_parts]
    w_refs = refs[n_parts:2 * n_parts]
    x_ref, g_ref, lng_ref, lnb_ref, o_ref = refs[2 * n_parts:]
    gate, ln_g, ln_b = g_ref[0], lng_ref[...], lnb_ref[...]
    for r0 in range(0, o_ref.shape[0], LN_ROW_SUB):
        rs = slice(r0, r0 + LN_ROW_SUB)
        mx = jnp.dot(y_refs[0][rs, :], w_refs[0][...], preferred_element_type=F32)
        for y_ref, w_ref in zip(y_refs[1:], w_refs[1:]):
            mx = mx + jnp.dot(y_ref[rs, :], w_ref[...], preferred_element_type=F32)
        r = DEEPNORM_ALPHA * x_ref[rs, :] + gate * mx
        o_ref[rs, :] = _layer_norm(r, ln_g, ln_b)


def _proj_ln(ys, ws, x, mod, ln_g, ln_b, *, gate_k, seq_rows, const_row):
    t = x.shape[0]
    tm = 512
    n_parts = len(ys)
    in_specs = [pl.BlockSpec((tm, y.shape[1]), lambda i: (i, 0)) for y in ys]
    in_specs += [pl.BlockSpec(w.shape, lambda i: (0, 0)) for w in ws]
    in_specs += [
        pl.BlockSpec((tm, D_MODEL), lambda i: (i, 0)),
        _mod_spec(gate_k, tm, seq_rows, const_row),
        pl.BlockSpec((1, D_MODEL), lambda i: (0, 0)),
        pl.BlockSpec((1, D_MODEL), lambda i: (0, 0)),
    ]
    return pl.pallas_call(
        functools.partial(_proj_ln_kernel, n_parts=n_parts),
        grid=(t // tm,),
        in_specs=in_specs,
        out_specs=pl.BlockSpec((tm, D_MODEL), lambda i: (i, 0)),
        out_shape=jax.ShapeDtypeStruct((t, D_MODEL), F32),
        compiler_params=_params(48, 1),
        name="proj_layer_norm",
    )(*ys, *ws, x, mod, ln_g.reshape(1, D_MODEL), ln_b.reshape(1, D_MODEL))


FFN_TF = 512


def _ffn_blocked(w):
    return w.astype(BF16).reshape(D_MODEL, FFN_HIDDEN // FFN_TF, FFN_TF).transpose(1, 0, 2)


def _ffn_kernel(x_ref, sh_ref, sc_ref, g_ref, wg_ref, wu_ref, wd_ref, lng_ref, lnb_ref, o_ref):
    f = pl.program_id(1)
    last = pl.num_programs(1) - 1
    tm = o_ref.shape[0]
    scale1 = 1.0 + sc_ref[0]
    shift = sh_ref[0]

    def hidden_chunk(rs):
        h = (x_ref[rs, :] * scale1 + shift).astype(BF16)
        acts = []
        for c0 in range(0, FFN_TF, V7X_MXU_DIM):
            gate = jnp.dot(h, wg_ref[0, :, c0:c0 + V7X_MXU_DIM], preferred_element_type=F32)
            up = jnp.dot(h, wu_ref[0, :, c0:c0 + V7X_MXU_DIM], preferred_element_type=F32)
            acts.append((gate * jax.nn.sigmoid(gate) * up).astype(BF16))
        return jnp.dot(jnp.concatenate(acts, axis=1), wd_ref[...], preferred_element_type=F32)

    @pl.when(f == 0)
    def _():
        for r0 in range(0, tm, ROW_SUB):
            rs = slice(r0, r0 + ROW_SUB)
            o_ref[rs, :] = hidden_chunk(rs)

    @pl.when((f > 0) & (f < last))
    def _():
        for r0 in range(0, tm, ROW_SUB):
            rs = slice(r0, r0 + ROW_SUB)
            o_ref[rs, :] += hidden_chunk(rs)

    @pl.when(f == last)
    def _():
        gate, ln_g, ln_b = g_ref[0], lng_ref[...], lnb_ref[...]
        for r0 in range(0, tm, ROW_SUB):
            rs = slice(r0, r0 + ROW_SUB)
            r = DEEPNORM_ALPHA * x_ref[rs, :] + gate * (o_ref[rs, :] + hidden_chunk(rs))
            o_ref[rs, :] = _layer_norm(r, ln_g, ln_b)


def _ffn_ln(x, mod, wg, wu, wd, ln_g, ln_b, *, seq_rows, const_row):
    t = x.shape[0]
    tm = 1024
    row = lambda i, f: (i, 0)
    vec = pl.BlockSpec((1, D_MODEL), lambda i, f: (0, 0))
    return pl.pallas_call(
        _ffn_kernel,
        grid=(t // tm, FFN_HIDDEN // FFN_TF),
        in_specs=[
            pl.BlockSpec((tm, D_MODEL), row),
            _mod_spec(3, tm, seq_rows, const_row),
            _mod_spec(4, tm, seq_rows, const_row),
            _mod_spec(5, tm, seq_rows, const_row),
            pl.BlockSpec((1, D_MODEL, FFN_TF), lambda i, f: (f, 0, 0)),
            pl.BlockSpec((1, D_MODEL, FFN_TF), lambda i, f: (f, 0, 0)),
            pl.BlockSpec((FFN_TF, D_MODEL), lambda i, f: (f, 0)),
            vec,
            vec,
        ],
        out_specs=pl.BlockSpec((tm, D_MODEL), row),
        out_shape=jax.ShapeDtypeStruct((t, D_MODEL), F32),
        compiler_params=_params(56, 2),
        name="swiglu_layer_norm",
    )(x, mod, mod, mod, wg, wu, wd, ln_g.reshape(1, D_MODEL), ln_b.reshape(1, D_MODEL))


def _rope_tables(n):
    rows = n // GRID_W
    row = jnp.repeat(jnp.arange(rows), GRID_W).astype(F32)
    col = jnp.tile(jnp.arange(GRID_W), rows).astype(F32)
    axis_dim = HEAD_DIM // 2
    inv_freq = ROPE_THETA ** (-jnp.arange(0, axis_dim, 2, dtype=F32) / axis_dim)
    ang_r = row[:, None] * inv_freq[None, :]
    ang_c = col[:, None] * inv_freq[None, :]
    cr, sr, cc, sc = jnp.cos(ang_r), jnp.sin(ang_r), jnp.cos(ang_c), jnp.sin(ang_c)
    zero = jnp.zeros_like(sr)
    cos = jnp.concatenate([cr, cr, cc, cc], axis=-1)
    sa = jnp.concatenate([-sr, zero, -sc, zero], axis=-1)
    sb = jnp.concatenate([zero, sr, zero, sc], axis=-1)
    return cos, sa, sb


def _attn_kernel(q_ref, kl_ref, vl_ref, kc_ref, vc_ref, lq1_ref, lk1_ref, lq2_ref, lk2_ref, g_ref,
                 o_ref, k_all, v_all, s_buf, mx_buf, w_buf, inv_buf, *, n, tq, lambda_init):
    k_all[0:CTX_LEN, :] = kc_ref[...]
    k_all[CTX_LEN:, :] = kl_ref[...]
    v_all[0:CTX_LEN, :] = vc_ref[...]
    v_all[CTX_LEN:, :] = vl_ref[...]
    lam = (jnp.exp(jnp.sum(lq1_ref[...] * lk1_ref[...], axis=-1, keepdims=True))
           - jnp.exp(jnp.sum(lq2_ref[...] * lk2_ref[...], axis=-1, keepdims=True)) + lambda_init)
    gain = g_ref[...] * (1.0 - lambda_init)
    nq = n // tq

    def rows(t):
        return pl.ds(pl.multiple_of(t * tq, tq), tq)

    def scores(t, slot):
        for m in range(2):
            lanes = slice(m * HEAD_DIM, (m + 1) * HEAD_DIM)
            s = lax.dot_general(q_ref[rows(t), lanes], k_all[:, lanes], (((1,), (1,)), ((), ())),
                                preferred_element_type=F32)
            s_buf[slot, m] = s
            mx_buf[slot, m] = jnp.max(s, axis=-1, keepdims=True)

    def softmax_diff(slot):
        sums = []
        for m in range(2):
            e = jnp.exp2(s_buf[slot, m] - mx_buf[slot, m])
            s_buf[slot, m] = e
            sums.append(jnp.sum(e, axis=-1, keepdims=True))
        ratio = lam * sums[0] / sums[1]
        w_buf[slot] = (s_buf[slot, 0] - s_buf[slot, 1] * ratio).astype(BF16)
        inv_buf[slot] = 1.0 / sums[0]

    def weighted_values(t, slot):
        o = jnp.dot(w_buf[slot], v_all[...], preferred_element_type=F32) * inv_buf[slot]
        o = o * lax.rsqrt(jnp.mean(o * o, axis=-1, keepdims=True) + RMS_EPS)
        o_ref[rows(t), :] = (o * gain).astype(o_ref.dtype)

    scores(0, 0)
    scores(1, 1)
    softmax_diff(0)

    def tile_pair(p, carry):
        t = 2 * p + 1
        scores(t + 1, 0)
        softmax_diff(1)
        weighted_values(t - 1, 0)
        scores(t + 2, 1)
        softmax_diff(0)
        weighted_values(t, 1)
        return carry

    lax.fori_loop(0, (nq - 2) // 2, tile_pair, 0)
    softmax_diff(1)
    weighted_values(nq - 2, 0)
    weighted_values(nq - 1, 1)


def _diff_attention(qk, v, kvc, lam_vecs, subln_g, n, lambda_init):
    t = qk.shape[0]
    tq = ATTN_TQ
    assert n % (2 * tq) == 0
    keys = CTX_LEN + n
    vec = pl.BlockSpec((1, HEAD_DIM), lambda b, h: (0, 0))
    return pl.pallas_call(
        functools.partial(_attn_kernel, n=n, tq=tq, lambda_init=lambda_init),
        grid=(t // n, N_HEADS),
        in_specs=[
            pl.BlockSpec((n, HEAD_V), lambda b, h: (b, h)),
            pl.BlockSpec((n, HEAD_V), lambda b, h: (b, N_HEADS + h)),
            pl.BlockSpec((n, HEAD_V), lambda b, h: (b, h)),
            pl.BlockSpec((CTX_LEN, HEAD_V), lambda b, h: (b, h)),
            pl.BlockSpec((CTX_LEN, HEAD_V), lambda b, h: (b, N_HEADS + h)),
            vec, vec, vec, vec,
            pl.BlockSpec((1, HEAD_V), lambda b, h: (0, 0)),
        ],
        out_specs=pl.BlockSpec((n, HEAD_V), lambda b, h: (b, h)),
        out_shape=jax.ShapeDtypeStruct((t, N_HEADS * HEAD_V), BF16),
        scratch_shapes=[
            pltpu.VMEM((keys, HEAD_V), BF16),
            pltpu.VMEM((keys, HEAD_V), BF16),
            pltpu.VMEM((2, 2, tq, keys), F32),
            pltpu.VMEM((2, 2, tq, 1), F32),
            pltpu.VMEM((2, tq, keys), BF16),
            pltpu.VMEM((2, tq, 1), F32),
        ],
        compiler_params=_params(48, 2),
        name="diff_attention",
    )(qk, qk, v, kvc, kvc, *[vv.reshape(1, HEAD_DIM) for vv in lam_vecs], subln_g.reshape(1, HEAD_V))


def _pool_fourier_mixer(h_src, mod, w_in, bands, pool_lin, pool_scale, fourier_lin, chan_tab, seq_tabs, n,
                        const_row):
    u = _modulated_matmul(h_src, mod, w_in, shift_k=0, scale_k=1, seq_rows=n, const_row=const_row)
    ya = _pool_mixer(u, bands, pool_lin, pool_scale, n)
    a, b = _chan_dft(u, chan_tab)
    yb = _seq_dft(seq_tabs[0], seq_tabs[1], a, b, fourier_lin, n)
    return ya, yb


def kernel(x, c, ctx, c_ctx, w_mod, b_mod, ln1_g, ln1_b, ln2_g, ln2_b, ffn_w_gate, ffn_w_up, ffn_w_down,
           pf_w_in, pool_lin, pool_scale, fourier_lin, pf_w_out,
           da_w_in, da_lam_q1, da_lam_k1, da_lam_q2, da_lam_k2, da_subln_g, da_w_out):
    batch, n, _ = x.shape
    n_ctx = ctx.shape[1]
    ctx_row = batch
    xs = x.reshape(batch * n, D_MODEL)
    cs = ctx.reshape(batch * n_ctx, D_MODEL)
    c_all = jnp.zeros((MOD_ROWS, D_MODEL), F32).at[:batch].set(c).at[ctx_row].set(c_ctx)

    chan_cos, chan_sin = _dft_tables(GROUP_DIM)
    chan_tab = jnp.concatenate([chan_cos, chan_sin], axis=1)
    seq_tabs = {n: _dft_tables(n), n_ctx: _dft_tables(n_ctx)}
    rope_tabs = _rope_tables(n)
    bands = _pool_bands()

    for l in range(DEPTH):
        last = l == DEPTH - 1
        j = l // 2
        mod = _modulation(c_all, w_mod[l], b_mod[l]).reshape(MOD_ROWS * N_MOD, 1, D_MODEL)
        wg, wu, wd = _ffn_blocked(ffn_w_gate[l]), _ffn_blocked(ffn_w_up[l]), ffn_w_down[l].astype(BF16)
        streams = [(xs, n, None)] + ([] if last else [(cs, n_ctx, ctx_row)])
        if l % 2 == 0:
            w_in = pf_w_in[j].astype(BF16)
            w_out = pf_w_out[j].astype(BF16)
            p_lin, f_lin = pool_lin[j].astype(BF16), fourier_lin[j].astype(BF16)
            mixed = []
            for src, rows, const_row in streams:
                ya, yb = _pool_fourier_mixer(src, mod, w_in, bands, p_lin, pool_scale[j], f_lin, chan_tab,
                                             seq_tabs[rows], rows, const_row)
                mixed.append(([ya, yb], [w_out[:POOL_WIDTH], w_out[POOL_WIDTH:]]))
        else:
            assert last, "context queries are only needed when a later layer consumes the context stream"
            lambda_init = 0.8 - 0.6 * math.exp(-0.3 * l)
            w_in = da_w_in[j].astype(BF16)
            w_out = da_w_out[j].astype(BF16)
            proj = functools.partial(_modulated_matmul, mod=mod, shift_k=0, scale_k=1)
            qk = proj(xs, w=w_in[:, :2 * QK_WIDTH], seq_rows=n, const_row=None, rope=rope_tabs,
                      q_cols=QK_WIDTH, q_scale=math.log2(math.e) / math.sqrt(HEAD_DIM))
            v = proj(xs, w=w_in[:, 2 * QK_WIDTH:], seq_rows=n, const_row=None)
            kvc = proj(cs, w=w_in[:, QK_WIDTH:], seq_rows=n_ctx, const_row=ctx_row)
            o_lat = _diff_attention(qk, v, kvc, (da_lam_q1[j], da_lam_k1[j], da_lam_q2[j], da_lam_k2[j]),
                                    da_subln_g[j], n, lambda_init)
            mixed = [([o_lat], [w_out])]
        new_streams = []
        for (src, rows, const_row), (ys, ws) in zip(streams, mixed):
            mid = _proj_ln(ys, ws, src, mod, ln1_g[l], ln1_b[l], gate_k=2, seq_rows=rows, const_row=const_row)
            new_streams.append(_ffn_ln(mid, mod, wg, wu, wd, ln2_g[l], ln2_b[l], seq_rows=rows,
                                       const_row=const_row))
        xs = new_streams[0]
        if not last:
            cs = new_streams[1]
    return xs.reshape(batch, n, D_MODEL)
```
